```python
import jax, jax.numpy as jnp
from jax import lax
import numpy as np

D_MODEL = 1024
BATCH = 8
SEQ = 4096
DEPTH = 1

HEAD_DIM = 64
RWKV_HEADS = D_MODEL // HEAD_DIM
RWKV_WIDTH = RWKV_HEADS * HEAD_DIM
DECAY_LORA = 64
AAA_LORA = 64
GATE_LORA = 160
RWKV_COLS = 3 * RWKV_WIDTH + DECAY_LORA + AAA_LORA + GATE_LORA
GN_EPS = 64e-5
ATTN_PAIRS = ((128, 1), (512, 4), (2048, 16))
ATTN_GROUPS = 3
ATTN_HEADS_PER_GROUP = 4
ATTN_HEADS = ATTN_GROUPS * ATTN_HEADS_PER_GROUP
ATTN_WIDTH = ATTN_HEADS * HEAD_DIM
N_BRANCHES = 2
IN_COLS = RWKV_COLS + 3 * ATTN_WIDTH + N_BRANCHES * D_MODEL
D_FF = 2816
RMS_EPS = 1e-6
NEG_INF = -1e30

kernel_name = 'hybrid_rwkv7_dilated_attn_macaron'


def _split_last(t, sizes):
    out, start = [], 0
    for n in sizes:
        out.append(t[..., start:start + n])
        start += n
    return out


def rms_norm(x, g):
    xf = x.astype(jnp.float32)
    y = xf * lax.rsqrt(jnp.mean(xf * xf, axis=-1, keepdims=True) + RMS_EPS)
    return y.astype(x.dtype) * g


def swiglu(h, w_in, w_out):
    gate, up = _split_last(h @ w_in, (D_FF, D_FF))
    return (jax.nn.silu(gate) * up) @ w_out


def token_shift(p, mu):
    prev = jnp.pad(p, ((0, 0), (1, 0), (0, 0)))[:, :-1]
    return p + (prev - p) * mu


def wkv7_scan(r, w, k, v, a, b):
    B, S, H, N = r.shape
    to_time = lambda t: jnp.moveaxis(t, 1, 0)

    def step(state, inp):
        r_t, w_t, k_t, v_t, a_t, b_t = inp
        sa = jnp.einsum('bhvk,bhk->bhv', state, a_t)
        state = (state * w_t[:, :, None, :] + sa[..., None] * b_t[:, :, None, :]
                 + v_t[..., None] * k_t[:, :, None, :])
        return state, jnp.einsum('bhvk,bhk->bhv', state, r_t)

    state0 = jnp.zeros((B, H, N, N), jnp.float32)
    _, out = lax.scan(step, state0, (to_time(r), to_time(w), to_time(k),
                                      to_time(v), to_time(a), to_time(b)))
    return jnp.moveaxis(out, 0, 1)


def rwkv7_time_mix(p, mu, w0, w2, a0, a2, g2, k_k, k_a, r_k, ln_w, ln_b):
    B, S, _ = p.shape
    H, N = RWKV_HEADS, HEAD_DIM
    p = token_shift(p.astype(jnp.float32), mu)
    r, k, v, wd, ad, gd = _split_last(
        p, (RWKV_WIDTH, RWKV_WIDTH, RWKV_WIDTH, DECAY_LORA, AAA_LORA, GATE_LORA))
    w = -jax.nn.softplus(-(w0 + jnp.tanh(wd) @ w2)) - 0.5
    decay = jnp.exp(-jnp.exp(w))
    a = jax.nn.sigmoid(a0 + ad @ a2)
    g = jax.nn.sigmoid(gd) @ g2
    heads = lambda t: t.reshape(B, S, H, N)
    kk = heads(k * k_k)
    kk = kk / jnp.maximum(jnp.sqrt(jnp.sum(kk * kk, axis=-1, keepdims=True)), 1e-12)
    k = k * (1.0 + (a - 1.0) * k_a)
    rh, kh, vh, ah = heads(r), heads(k), heads(v), heads(a)
    wkv = wkv7_scan(rh, heads(decay), kh, vh, -kk, kk * ah)
    mean = jnp.mean(wkv, axis=-1, keepdims=True)
    var = jnp.mean(jnp.square(wkv - mean), axis=-1, keepdims=True)
    y = ((wkv - mean) * lax.rsqrt(var + GN_EPS)).reshape(B, S, RWKV_WIDTH) * ln_w + ln_b
    bonus = jnp.sum(rh * kh * r_k, axis=-1, keepdims=True) * vh
    return (y + bonus.reshape(B, S, RWKV_WIDTH)) * g


def dilated_group_attention(q, k, v, window, dilation):
    B, S, Hg, E = q.shape
    span = window // dilation
    blk = span
    sub = S // dilation
    nb = -(-sub // blk)
    pad = nb * blk - sub

    def to_sub(t):
        t = t.reshape(B, sub, dilation, Hg, E).transpose(0, 2, 1, 3, 4)
        t = jnp.pad(t, ((0, 0), (0, 0), (0, pad), (0, 0), (0, 0)))
        return t.reshape(B, dilation, nb, blk, Hg, E)

    def with_prev(t):
        tp = jnp.pad(t, ((0, 0), (0, 0), (1, 0), (0, 0), (0, 0), (0, 0)))
        return jnp.concatenate([tp[:, :, :-1], tp[:, :, 1:]], axis=3)

    qb = to_sub(q)
    kw, vw = with_prev(to_sub(k)), with_prev(to_sub(v))
    s = jnp.einsum('bdnqhe,bdnkhe->bdnhqk', qb, kw)
    qi = jnp.arange(blk)[:, None]
    kj = jnp.arange(2 * blk)[None, :]
    dist = qi + blk - kj
    bidx = jnp.arange(nb)[:, None, None]
    valid = (dist >= 0) & (dist <= span) & (bidx * blk + kj - blk >= 0)
    s = jnp.where(valid[None, None, :, None], s, NEG_INF)
    m = jnp.max(s, axis=-1, keepdims=True)
    pexp = jnp.exp(s - m)
    den = jnp.sum(pexp, axis=-1, keepdims=True)
    o = jnp.einsum('bdnhqk,bdnkhe->bdnqhe', pexp / den, vw)
    lse = (m + jnp.log(den))[..., 0].transpose(0, 1, 2, 4, 3)
    o = o.reshape(B, dilation, nb * blk, Hg, E)[:, :, :sub]
    o = o.transpose(0, 2, 1, 3, 4).reshape(B, S, Hg, E)
    lse = lse.reshape(B, dilation, nb * blk, Hg)[:, :, :sub]
    lse = lse.transpose(0, 2, 1, 3).reshape(B, S, Hg)
    return o, lse


def dilated_attention(pq, pk, pv, q_gain, k_gain):
    B, S, _ = pq.shape
    heads = lambda t: t.astype(jnp.float32).reshape(B, S, ATTN_HEADS, HEAD_DIM)
    q = rms_norm(heads(pq), q_gain) * (HEAD_DIM ** -0.5)
    k = rms_norm(heads(pk), k_gain)
    v = heads(pv)
    outs, lses = [], []
    for gi, (window, dilation) in enumerate(ATTN_PAIRS):
        sl = slice(gi * ATTN_HEADS_PER_GROUP, (gi + 1) * ATTN_HEADS_PER_GROUP)
        o, lse = dilated_group_attention(q[:, :, sl], k[:, :, sl], v[:, :, sl], window, dilation)
        outs.append(o)
        lses.append(lse)
    o = jnp.stack(outs, axis=2)
    alpha = jax.nn.softmax(jnp.stack(lses, axis=2), axis=2)
    return (o * alpha[..., None]).reshape(B, S, ATTN_WIDTH)


def setup_inputs(seed: int = 0) -> dict:
    key = jax.random.key(seed)
    ks = jax.random.split(key, 27)
    f32 = jnp.float32
    L, D = DEPTH, D_MODEL

    def nrm(k, shape, scale):
        return scale * jax.random.normal(k, shape, f32)

    def gain(k, shape):
        return 1.0 + 0.1 * jax.random.normal(k, shape, f32)

    return {
        'x': jax.random.normal(ks[0], (BATCH, SEQ, D), f32),
        'ffn1_norm': gain(ks[1], (L, D)),
        'ffn1_w_in': nrm(ks[2], (L, D, 2 * D_FF), D ** -0.5),
        'ffn1_w_out': nrm(ks[3], (L, D_FF, D), D_FF ** -0.5),
        'mix_norm': gain(ks[4], (L, D)),
        'w_in': nrm(ks[5], (L, D, IN_COLS), D ** -0.5),
        'b_gate': nrm(ks[6], (L, N_BRANCHES * D), 0.1),
        'rwkv_mu': jax.random.uniform(ks[7], (L, RWKV_COLS), f32),
        'rwkv_w0': jax.random.uniform(ks[8], (L, RWKV_WIDTH), f32, -6.0, 0.0),
        'rwkv_w2': nrm(ks[9], (L, DECAY_LORA, RWKV_WIDTH), 0.1 * DECAY_LORA ** -0.5),
        'rwkv_a0': nrm(ks[10], (L, RWKV_WIDTH), 0.1),
        'rwkv_a2': nrm(ks[11], (L, AAA_LORA, RWKV_WIDTH), AAA_LORA ** -0.5),
        'rwkv_g2': nrm(ks[12], (L, GATE_LORA, RWKV_WIDTH), GATE_LORA ** -0.5),
        'rwkv_k_k': gain(ks[13], (L, RWKV_WIDTH)),
        'rwkv_k_a': gain(ks[14], (L, RWKV_WIDTH)),
        'rwkv_r_k': nrm(ks[15], (L, RWKV_HEADS, HEAD_DIM), 0.1),
        'rwkv_ln_w': gain(ks[16], (L, RWKV_WIDTH)),
        'rwkv_ln_b': nrm(ks[17], (L, RWKV_WIDTH), 0.01),
        'attn_q_norm': gain(ks[18], (L, HEAD_DIM)),
        'attn_k_norm': gain(ks[19], (L, HEAD_DIM)),
        'w_proj_rwkv': nrm(ks[20], (L, RWKV_WIDTH, D), RWKV_WIDTH ** -0.5),
        'w_proj_attn': nrm(ks[21], (L, ATTN_WIDTH, D), ATTN_WIDTH ** -0.5),
        'w_out': nrm(ks[22], (L, D, D), D ** -0.5),
        'ffn2_norm': gain(ks[23], (L, D)),
        'ffn2_w_in': nrm(ks[24], (L, D, 2 * D_FF), D ** -0.5),
        'ffn2_w_out': nrm(ks[25], (L, D_FF, D), D_FF ** -0.5),
    }


def reference(x, ffn1_norm, ffn1_w_in, ffn1_w_out, mix_norm, w_in, b_gate, rwkv_mu,
              rwkv_w0, rwkv_w2, rwkv_a0, rwkv_a2, rwkv_g2, rwkv_k_k, rwkv_k_a, rwkv_r_k,
              rwkv_ln_w, rwkv_ln_b, attn_q_norm, attn_k_norm, w_proj_rwkv, w_proj_attn,
              w_out, ffn2_norm, ffn2_w_in, ffn2_w_out):
    for l in range(DEPTH):
        x = x + 0.5 * swiglu(rms_norm(x, ffn1_norm[l]), ffn1_w_in[l], ffn1_w_out[l])
        h = rms_norm(x, mix_norm[l])
        p_rwkv, p_q, p_k, p_v, p_gate = _split_last(
            h @ w_in[l], (RWKV_COLS, ATTN_WIDTH, ATTN_WIDTH, ATTN_WIDTH, N_BRANCHES * D_MODEL))
        y_a = rwkv7_time_mix(p_rwkv, rwkv_mu[l], rwkv_w0[l], rwkv_w2[l], rwkv_a0[l],
                             rwkv_a2[l], rwkv_g2[l], rwkv_k_k[l], rwkv_k_a[l], rwkv_r_k[l],
                             rwkv_ln_w[l], rwkv_ln_b[l]).astype(x.dtype)
        y_b = dilated_attention(p_q, p_k, p_v, attn_q_norm[l], attn_k_norm[l]).astype(x.dtype)
        g_a, g_b = _split_last(jax.nn.sigmoid(p_gate + b_gate[l]), (D_MODEL, D_MODEL))
        merged = g_a * (y_a @ w_proj_rwkv[l]) + g_b * (y_b @ w_proj_attn[l])
        x = x + merged @ w_out[l]
        x = x + 0.5 * swiglu(rms_norm(x, ffn2_norm[l]), ffn2_w_in[l], ffn2_w_out[l])
    return x
```

```python
import functools

import jax
import jax.numpy as jnp
from jax import lax
from jax.experimental import pallas as pl
from jax.experimental.pallas import tpu as pltpu

F32 = jnp.float32
BF16 = jnp.bfloat16

D_MODEL = 1024
HEAD_DIM = 64
RWKV_WIDTH = 1024
DECAY_LORA = 64
AAA_LORA = 64
GATE_LORA = 160
RWKV_COLS = 3 * RWKV_WIDTH + DECAY_LORA + AAA_LORA + GATE_LORA
GN_EPS = 64e-5
ATTN_PAIRS = ((128, 1), (512, 4), (2048, 16))
ATTN_GROUPS = 3
ATTN_GROUP_WIDTH = 256
ATTN_WIDTH = ATTN_GROUPS * ATTN_GROUP_WIDTH
ATTN_BLOCK = 128
D_FF = 2816
RMS_EPS = 1e-6
NEG_INF = -1e30

LANES = 128
MXU_DIM = 256
HEADS_PER_TILE = MXU_DIM // HEAD_DIM
VMEM_LIMIT_BYTES = 56 * 1024 * 1024

LORA_PAD = 128
GATE_LORA_PAD = 256
RW_PAD = 3 * RWKV_WIDTH + LORA_PAD + GATE_LORA_PAD
ATT0 = RW_PAD
GATE0 = ATT0 + 3 * ATTN_WIDTH
IN_PAD = GATE0 + 2 * D_MODEL

FF_CHUNK = D_FF // 2
FFN_TILE = 512
PROJ_TILE = 256
MERGE_TILE = 256
WKV_CHUNK = 64
WKV_LOG2_CHUNK = 6


def _dot(a, b):
    return jnp.dot(a, b, preferred_element_type=F32)


def _dot_nt(a, b):
    return lax.dot_general(a, b, (((1,), (1,)), ((), ())), preferred_element_type=F32)


def _dot_tn(a, b):
    return lax.dot_general(a, b, (((0,), (0,)), ((), ())), preferred_element_type=F32)


def _rms_norm_bf16(x, gain):
    ms = jnp.mean(x * x, axis=-1, keepdims=True)
    return (x * lax.rsqrt(ms + RMS_EPS) * gain).astype(BF16)


def _head_sums(t, ones_bd):
    parts = []
    for q in range(t.shape[-1] // MXU_DIM):
        parts.append(_dot(t[:, q * MXU_DIM:(q + 1) * MXU_DIM].astype(BF16), ones_bd))
    return jnp.concatenate(parts, axis=-1)


def _resident(shape):
    return pl.BlockSpec(shape, lambda *_: (0,) * len(shape), pipeline_mode=pl.Buffered(1))


def _ffn_body(x_ref, gain_ref, w_in_ref, w_out_ref, o_ref):
    x = x_ref[...]
    h = _rms_norm_bf16(x, gain_ref[...])
    acc = None
    for c in range(D_FF // FF_CHUNK):
        lo = c * FF_CHUNK
        gate = _dot(h, w_in_ref[:, lo:lo + FF_CHUNK])
        up = _dot(h, w_in_ref[:, D_FF + lo:D_FF + lo + FF_CHUNK])
        act = (gate * jax.nn.sigmoid(gate) * up).astype(BF16)
        part = _dot(act, w_out_ref[lo:lo + FF_CHUNK, :])
        acc = part if acc is None else acc + part
    o_ref[...] = x + 0.5 * acc


def _ffn(x, gain, w_in, w_out):
    t = x.shape[0]
    tm = FFN_TILE
    return pl.pallas_call(
        _ffn_body,
        grid=(t // tm,),
        in_specs=[
            pl.BlockSpec((tm, D_MODEL), lambda i: (i, 0)),
            _resident((1, D_MODEL)),
            _resident((D_MODEL, 2 * D_FF)),
            _resident((D_FF, D_MODEL)),
        ],
        out_specs=pl.BlockSpec((tm, D_MODEL), lambda i: (i, 0)),
        out_shape=jax.ShapeDtypeStruct((t, D_MODEL), F32),
        compiler_params=pltpu.CompilerParams(
            dimension_semantics=("parallel",), vmem_limit_bytes=VMEM_LIMIT_BYTES),
        name="ffn",
    )(x, gain, w_in, w_out)


def _softplus(z):
    return jnp.maximum(z, 0.0) + jnp.log(1.0 + jnp.exp(-jnp.abs(z)))


def _proj_body(tiles_per_seq,
               x_ref, gain_ref, w_ref, bg_ref, mu_ref, w0_ref, w2_ref, a0_ref, a2_ref, g2_ref,
               kk_ref, ka_ref, rk_ref, qg_ref, kg_ref, ones_ref,
               r_o, lw_o, k_o, v_o, kkn_o, b_o, bonus_o, g_o, aq_o, ak_o, av_o, gate_o,
               carry_ref):
    i = pl.program_id(0)
    tm = x_ref.shape[0]
    w = RWKV_WIDTH
    h = _rms_norm_bf16(x_ref[...], gain_ref[...])
    ones_bd = ones_ref[...]

    p = _dot(h, w_ref[:, :RW_PAD])
    @pl.when((i % tiles_per_seq) == 0)
    def _():
        carry_ref[...] = jnp.zeros_like(carry_ref)

    row = lax.broadcasted_iota(jnp.int32, (tm, 1), 0)
    prev = jnp.where(row == 0, carry_ref[...], pltpu.roll(p, 1, 0))
    carry_ref[...] = p[tm - 1:tm, :]
    ps = p + (prev - p) * mu_ref[...]

    r = ps[:, 0:w]
    k0 = ps[:, w:2 * w]
    v = ps[:, 2 * w:3 * w]
    lora = ps[:, 3 * w:3 * w + LORA_PAD]
    gd = ps[:, 3 * w + LORA_PAD:RW_PAD]

    wraw = w0_ref[...] + _dot(jnp.tanh(lora).astype(BF16), w2_ref[...])
    lw = -jnp.exp(-_softplus(-wraw) - 0.5)
    a = jax.nn.sigmoid(a0_ref[...] + _dot(lora.astype(BF16), a2_ref[...]))
    g = _dot(jax.nn.sigmoid(gd).astype(BF16), g2_ref[...])

    kk = k0 * kk_ref[...]
    kkn = kk * lax.rsqrt(jnp.maximum(_head_sums(kk * kk, ones_bd), 1e-24))
    k = k0 * (1.0 + (a - 1.0) * ka_ref[...])
    bonus = _head_sums(r * k * rk_ref[...], ones_bd) * v

    r_o[...] = r
    lw_o[...] = lw
    k_o[...] = k
    v_o[...] = v
    kkn_o[...] = kkn
    b_o[...] = kkn * a
    bonus_o[...] = bonus
    g_o[...] = g

    pa = _dot(h, w_ref[:, ATT0:GATE0])
    pq = pa[:, 0:ATTN_WIDTH]
    pk = pa[:, ATTN_WIDTH:2 * ATTN_WIDTH]
    inv_n = 1.0 / HEAD_DIM
    qn = pq * lax.rsqrt(_head_sums(pq * pq, ones_bd) * inv_n + RMS_EPS) * qg_ref[...]
    kn = pk * lax.rsqrt(_head_sums(pk * pk, ones_bd) * inv_n + RMS_EPS) * kg_ref[...]
    aq_o[...] = qn.astype(BF16)
    ak_o[...] = kn.astype(BF16)
    av_o[...] = pa[:, 2 * ATTN_WIDTH:].astype(BF16)

    gate_o[...] = jax.nn.sigmoid(_dot(h, w_ref[:, GATE0:]) + bg_ref[...])


def _proj(x, seq_len, gain, w, bg, mu, w0, w2, a0, a2, g2, k_k, k_a, r_k, qg, kg, ones_bd):
    t = x.shape[0]
    tm = PROJ_TILE
    row = lambda n: pl.BlockSpec((tm, n), lambda i: (i, 0))
    wide = jax.ShapeDtypeStruct((t, RWKV_WIDTH), F32)
    att = jax.ShapeDtypeStruct((t, ATTN_WIDTH), BF16)
    return pl.pallas_call(
        functools.partial(_proj_body, seq_len // tm),
        grid=(t // tm,),
        in_specs=[row(D_MODEL)] + [_resident(a.shape) for a in
                                   (gain, w, bg, mu, w0, w2, a0, a2, g2, k_k, k_a, r_k, qg, kg, ones_bd)],
        out_specs=[row(RWKV_WIDTH)] * 8 + [row(ATTN_WIDTH)] * 3 + [row(2 * D_MODEL)],
        out_shape=[wide] * 8 + [att] * 3 + [jax.ShapeDtypeStruct((t, 2 * D_MODEL), F32)],
        scratch_shapes=[pltpu.VMEM((1, RW_PAD), F32)],
        compiler_params=pltpu.CompilerParams(
            dimension_semantics=("arbitrary",), vmem_limit_bytes=VMEM_LIMIT_BYTES),
        name="proj",
    )(x, gain, w, bg, mu, w0, w2, a0, a2, g2, k_k, k_a, r_k, qg, kg, ones_bd)


def _wkv_body(r_ref, lw_ref, k_ref, v_ref, kkn_ref, b_ref, y_ref, z_ref):
    c_len = WKV_CHUNK
    n_tiles = RWKV_WIDTH // MXU_DIM

    @pl.when(pl.program_id(1) == 0)
    def _():
        z_ref[...] = jnp.zeros_like(z_ref)

    lw = lw_ref[...]
    ri = lax.broadcasted_iota(jnp.int32, (c_len, c_len), 0)
    ci = lax.broadcasted_iota(jnp.int32, (c_len, c_len), 1)
    tri = jnp.where(ci <= ri, 1.0, 0.0).astype(BF16)
    hi = lw.astype(BF16)
    rem = lw - hi.astype(F32)
    mid = rem.astype(BF16)
    lo = (rem - mid.astype(F32)).astype(BF16)
    cum = _dot(tri, hi) + _dot(tri, mid) + _dot(tri, lo)

    ref = cum[c_len // 2 - 1:c_len // 2, :]
    e_fwd = jnp.exp(cum - ref)
    e_prev = jnp.exp(cum - lw - ref)
    e_bwd = jnp.exp(ref - cum)
    e_ref = jnp.exp(ref)
    lam_end = e_fwd[c_len - 1:c_len, :]
    lam_c = jnp.exp(cum[c_len - 1:c_len, :])

    at = -kkn_ref[...] * e_prev
    rt = r_ref[...] * e_fwd
    kt = k_ref[...] * e_bwd
    bt = b_ref[...] * e_bwd
    v = v_ref[...]
    at_s = at * e_ref
    rt_s = rt * e_ref
    kh = kt * lam_end
    bh = bt * lam_end

    lane = lax.broadcasted_iota(jnp.int32, (c_len, MXU_DIM), 1)
    rowc = lax.broadcasted_iota(jnp.int32, (c_len, MXU_DIM), 0)
    head_mask = [(lane // HEAD_DIM) == h for h in range(HEADS_PER_TILE)]
    blk_mask = [(lane // c_len) == h for h in range(HEADS_PER_TILE)]
    col_in_blk = lane % c_len
    strict = col_in_blk < rowc
    incl = col_in_blk <= rowc
    eye = jnp.where(col_in_blk == rowc, 1.0, 0.0)
    br = lax.broadcasted_iota(jnp.int32, (MXU_DIM, MXU_DIM), 0) // HEAD_DIM
    bc = lax.broadcasted_iota(jnp.int32, (MXU_DIM, MXU_DIM), 1) // HEAD_DIM
    bd_mask = br == bc

    def stack(x, masks):
        return jnp.concatenate([jnp.where(m, x, 0.0) for m in masks], axis=0).astype(BF16)

    for q in range(n_tiles):
        sl = slice(q * MXU_DIM, (q + 1) * MXU_DIM)
        ar = jnp.concatenate([at[:, sl], rt[:, sl]], axis=0).astype(BF16)
        sb = _dot_nt(ar, stack(bt[:, sl], head_mask))
        sk = _dot_nt(ar, stack(kt[:, sl], head_mask))
        a_ab = jnp.where(strict, sb[:c_len], 0.0)
        a_rb = jnp.where(incl, sb[c_len:], 0.0)
        a_ak = jnp.where(strict, sk[:c_len], 0.0)
        a_rk = jnp.where(incl, sk[c_len:], 0.0)

        pw = _dot(a_ab.astype(BF16), stack(a_ab, blk_mask))
        t_inv = eye + a_ab
        for lev in range(1, WKV_LOG2_CHUNK):
            rhs = stack(pw, blk_mask)
            if lev < WKV_LOG2_CHUNK - 1:
                prod = _dot(jnp.concatenate([pw, t_inv], axis=0).astype(BF16), rhs)
                pw = prod[:c_len]
                t_inv = t_inv + prod[c_len:]
            else:
                t_inv = t_inv + _dot(t_inv.astype(BF16), rhs)
        t16 = t_inv.astype(BF16)

        vq = v[:, sl]
        xv = _dot(jnp.concatenate([a_ak, a_rk], axis=0).astype(BF16), stack(vq, head_mask))
        wt = _dot(t16, stack(at_s[:, sl], head_mask))
        u0 = _dot(t16, stack(xv[:c_len], head_mask))
        a_rb16 = a_rb.astype(BF16)
        qh = rt_s[:, sl] + _dot(a_rb16, stack(wt, head_mask))
        y0 = _dot(a_rb16, stack(u0, head_mask)) + xv[c_len:]

        z = z_ref[q]
        z16 = z.astype(BF16)
        y_ref[:, sl] = _dot_nt(qh.astype(BF16), z16) + y0

        m_lr = _dot_tn(wt.astype(BF16), bh[:, sl].astype(BF16))
        uv = jnp.concatenate([u0, vq], axis=0).astype(BF16)
        bk = jnp.concatenate([bh[:, sl], kh[:, sl]], axis=0).astype(BF16)
        gq = _dot_tn(uv, bk)
        z_ref[q] = (z * lam_c[:, sl]
                    + _dot(z16, jnp.where(bd_mask, m_lr, 0.0).astype(BF16))
                    + jnp.where(bd_mask, gq, 0.0))


def _wkv(batch, seq_len, r, lw, k, v, kkn, b):
    t = r.shape[0]
    n_chunks = seq_len // WKV_CHUNK
    blk = pl.BlockSpec((WKV_CHUNK, RWKV_WIDTH), lambda bi, ci: (bi * n_chunks + ci, 0))
    return pl.pallas_call(
        _wkv_body,
        grid=(batch, n_chunks),
        in_specs=[blk] * 6,
        out_specs=blk,
        out_shape=jax.ShapeDtypeStruct((t, RWKV_WIDTH), F32),
        scratch_shapes=[pltpu.VMEM((RWKV_WIDTH // MXU_DIM, MXU_DIM, MXU_DIM), F32)],
        compiler_params=pltpu.CompilerParams(
            dimension_semantics=("arbitrary", "arbitrary"), vmem_limit_bytes=VMEM_LIMIT_BYTES),
        name="wkv",
    )(r, lw, k, v, kkn, b)


def _attn_body(blocks_per_sub, *refs):
    n_in = 5 * ATTN_GROUPS
    in_refs, out_refs = refs[:n_in], refs[n_in:]
    i = pl.program_id(1)
    blk = ATTN_BLOCK
    qi = lax.broadcasted_iota(jnp.int32, (HEADS_PER_TILE * blk, 2 * blk), 0) % blk
    kj = lax.broadcasted_iota(jnp.int32, (HEADS_PER_TILE * blk, 2 * blk), 1)
    dist = qi + blk - kj
    window = (dist >= 0) & (dist <= blk)
    lane = lax.broadcasted_iota(jnp.int32, (blk, ATTN_GROUP_WIDTH), 1)
    head_mask = [(lane // HEAD_DIM) == h for h in range(HEADS_PER_TILE)]

    for g in range(ATTN_GROUPS):
        q_ref, kp_ref, kc_ref, vp_ref, vc_ref = in_refs[5 * g:5 * g + 5]
        o_ref, l_ref = out_refs[2 * g:2 * g + 2]
        has_prev = (i % blocks_per_sub[g]) > 0
        valid = window & (has_prev | (kj >= blk))
        q = q_ref[...]
        qs = jnp.concatenate([jnp.where(m, q, jnp.zeros_like(q)) for m in head_mask], axis=0)
        kcat = jnp.concatenate([kp_ref[...], kc_ref[...]], axis=0)
        vcat = jnp.concatenate([vp_ref[...], vc_ref[...]], axis=0)
        s = jnp.where(valid, _dot_nt(qs, kcat), NEG_INF)
        m = jnp.max(s, axis=-1, keepdims=True)
        p = jnp.exp(s - m)
        den = jnp.sum(p, axis=-1, keepdims=True)
        o_all = _dot(p.astype(BF16), vcat) / den
        lse = m + jnp.log(den)
        o = jnp.zeros((blk, ATTN_GROUP_WIDTH), F32)
        l = jnp.zeros((blk, ATTN_GROUP_WIDTH), F32)
        for h in range(HEADS_PER_TILE):
            rows = slice(h * blk, (h + 1) * blk)
            o = jnp.where(head_mask[h], o_all[rows], o)
            l = jnp.where(head_mask[h], lse[rows], l)
        o_ref[...] = o
        l_ref[...] = l


def _attn(batch, seq_len, aq, ak, av):
    blk = ATTN_BLOCK
    n_steps = seq_len // blk
    in_specs, args, out_specs, out_shape, blocks_per_sub = [], [], [], [], []
    for g, (_, dil) in enumerate(ATTN_PAIRS):
        sub = seq_len // dil
        nb = sub // blk
        blocks_per_sub.append(nb)

        def cur(bi, i, g=g, nb=nb):
            return (bi, i % nb, (i // nb) * ATTN_GROUPS + g)

        def prev(bi, i, g=g, nb=nb):
            return (bi, jnp.maximum(i % nb - 1, 0), (i // nb) * ATTN_GROUPS + g)

        def out(bi, i, nb=nb):
            return (bi, i % nb, i // nb)

        view = lambda a: a.reshape(batch, sub, dil * ATTN_WIDTH)
        spec = lambda f: pl.BlockSpec((None, blk, ATTN_GROUP_WIDTH), f)
        in_specs += [spec(cur), spec(prev), spec(cur), spec(prev), spec(cur)]
        args += [view(aq), view(ak), view(ak), view(av), view(av)]
        out_specs += [spec(out), spec(out)]
        out_shape += [jax.ShapeDtypeStruct((batch, sub, dil * ATTN_GROUP_WIDTH), F32)] * 2
    outs = pl.pallas_call(
        functools.partial(_attn_body, tuple(blocks_per_sub)),
        grid=(batch, n_steps),
        in_specs=in_specs,
        out_specs=out_specs,
        out_shape=out_shape,
        compiler_params=pltpu.CompilerParams(
            dimension_semantics=("parallel", "parallel"), vmem_limit_bytes=VMEM_LIMIT_BYTES),
        name="attn",
    )(*args)
    return [o.reshape(batch * seq_len, ATTN_GROUP_WIDTH) for o in outs]


def _merge_body(x_ref, wkv_ref, bonus_ref, g_ref, gate_ref,
                o0_ref, l0_ref, o1_ref, l1_ref, o2_ref, l2_ref,
                lnw_ref, lnb_ref, ones_ref, wpr_ref, wpa_ref, wo_ref, out_ref):
    ones_bd = ones_ref[...]
    inv_n = 1.0 / HEAD_DIM
    wkv = wkv_ref[...]
    dev = wkv - _head_sums(wkv, ones_bd) * inv_n
    var = _head_sums(dev * dev, ones_bd) * inv_n
    y = dev * lax.rsqrt(var + GN_EPS) * lnw_ref[...] + lnb_ref[...]
    y_a = ((y + bonus_ref[...]) * g_ref[...]).astype(BF16)
    pa = _dot(y_a, wpr_ref[...])
    l0, l1, l2 = l0_ref[...], l1_ref[...], l2_ref[...]
    mx = jnp.maximum(jnp.maximum(l0, l1), l2)
    e0, e1, e2 = jnp.exp(l0 - mx), jnp.exp(l1 - mx), jnp.exp(l2 - mx)
    inv = 1.0 / (e0 + e1 + e2)
    pb = None
    for gi, (o_ref, e) in enumerate(((o0_ref, e0), (o1_ref, e1), (o2_ref, e2))):
        yb = (o_ref[...] * (e * inv)).astype(BF16)
        part = _dot(yb, wpa_ref[gi * ATTN_GROUP_WIDTH:(gi + 1) * ATTN_GROUP_WIDTH, :])
        pb = part if pb is None else pb + part
    gates = gate_ref[...]
    merged = gates[:, :D_MODEL] * pa + gates[:, D_MODEL:] * pb
    out_ref[...] = x_ref[...] + _dot(merged.astype(BF16), wo_ref[...])


def _merge(x, wkv, bonus, g, gates, attn_outs, ln_w, ln_b, ones_bd, wpr, wpa, wo):
    t = x.shape[0]
    tm = MERGE_TILE
    row = lambda n: pl.BlockSpec((tm, n), lambda i: (i, 0))
    small = (ln_w, ln_b, ones_bd, wpr, wpa, wo)
    return pl.pallas_call(
        _merge_body,
        grid=(t // tm,),
        in_specs=[row(D_MODEL)] * 4 + [row(2 * D_MODEL)] + [row(ATTN_GROUP_WIDTH)] * 6
                 + [_resident(a.shape) for a in small],
        out_specs=row(D_MODEL),
        out_shape=jax.ShapeDtypeStruct((t, D_MODEL), F32),
        compiler_params=pltpu.CompilerParams(
            dimension_semantics=("parallel",), vmem_limit_bytes=VMEM_LIMIT_BYTES),
        name="merge",
    )(x, wkv, bonus, g, gates, *attn_outs, *small)


def _pad_rows(a, n):
    return jnp.pad(a, ((0, n - a.shape[0]), (0, 0)))


def _layer(x, seq_len, ffn1_norm, ffn1_w_in, ffn1_w_out, mix_norm, w_in, b_gate, rwkv_mu,
           rwkv_w0, rwkv_w2, rwkv_a0, rwkv_a2, rwkv_g2, rwkv_k_k, rwkv_k_a, rwkv_r_k,
           rwkv_ln_w, rwkv_ln_b, attn_q_norm, attn_k_norm, w_proj_rwkv, w_proj_attn,
           w_out, ffn2_norm, ffn2_w_in, ffn2_w_out):
    batch = x.shape[0] // seq_len
    row = lambda a: a.reshape(1, -1)
    w3 = 3 * RWKV_WIDTH
    lora_cols = DECAY_LORA + AAA_LORA

    zeros = lambda n: jnp.zeros((D_MODEL, n), w_in.dtype)
    w_pad = jnp.concatenate([
        w_in[:, :w3 + lora_cols + GATE_LORA], zeros(GATE_LORA_PAD - GATE_LORA),
        w_in[:, RWKV_COLS:]], axis=1).astype(BF16)
    mu_pad = jnp.pad(rwkv_mu, (0, RW_PAD - RWKV_COLS)).reshape(1, RW_PAD)
    w2_pad = _pad_rows(rwkv_w2, LORA_PAD).astype(BF16)
    a2_pad = jnp.pad(rwkv_a2, ((DECAY_LORA, 0), (0, 0))).astype(BF16)
    g2_pad = _pad_rows(rwkv_g2, GATE_LORA_PAD).astype(BF16)
    n_heads = ATTN_WIDTH // HEAD_DIM
    q_gain = row(jnp.tile(attn_q_norm, n_heads) * HEAD_DIM ** -0.5)
    k_gain = row(jnp.tile(attn_k_norm, n_heads))
    idx = jnp.arange(MXU_DIM) // HEAD_DIM
    ones_bd = (idx[:, None] == idx[None, :]).astype(BF16)

    x = _ffn(x, row(ffn1_norm), ffn1_w_in.astype(BF16), ffn1_w_out.astype(BF16))
    (r, lw, k, v, kkn, b, bonus, g, aq, ak, av, gates) = _proj(
        x, seq_len, row(mix_norm), w_pad, row(b_gate), mu_pad, row(rwkv_w0), w2_pad,
        row(rwkv_a0), a2_pad, g2_pad, row(rwkv_k_k), row(rwkv_k_a), row(rwkv_r_k),
        q_gain, k_gain, ones_bd)
    wkv = _wkv(batch, seq_len, r, lw, k, v, kkn, b)
    attn_outs = _attn(batch, seq_len, aq, ak, av)
    x = _merge(x, wkv, bonus, g, gates, attn_outs, row(rwkv_ln_w), row(rwkv_ln_b), ones_bd,
               w_proj_rwkv.astype(BF16), w_proj_attn.astype(BF16), w_out.astype(BF16))
    return _ffn(x, row(ffn2_norm), ffn2_w_in.astype(BF16), ffn2_w_out.astype(BF16))


def kernel(x, ffn1_norm, ffn1_w_in, ffn1_w_out, mix_norm, w_in, b_gate, rwkv_mu, rwkv_w0, rwkv_w2, rwkv_a0, rwkv_a2, rwkv_g2, rwkv_k_k, rwkv_k_a, rwkv_r_k, rwkv_ln_w, rwkv_ln_b, attn_q_norm, attn_k_norm, w_proj_rwkv, w_proj_attn, w_out, ffn2_norm, ffn2_w_in, ffn2_w_out):
    batch, seq_len, d = x.shape
    params = (ffn1_norm, ffn1_w_in, ffn1_w_out, mix_norm, w_in, b_gate, rwkv_mu, rwkv_w0, rwkv_w2,
              rwkv_a0, rwkv_a2, rwkv_g2, rwkv_k_k, rwkv_k_a, rwkv_r_k, rwkv_ln_w, rwkv_ln_b,
              attn_q_norm, attn_k_norm, w_proj_rwkv, w_proj_attn, w_out, ffn2_norm, ffn2_w_in,
              ffn2_w_out)
    h = x.reshape(batch * seq_len, d)
    for layer in range(ffn1_norm.shape[0]):
        h = _layer(h, seq_len, *(p[layer] for p in params))
    return h.reshape(batch, seq_len, d)
```

```python
import functools

import jax
import jax.numpy as jnp
from jax import lax
from jax.experimental import pallas as pl
from jax.experimental.pallas import tpu as pltpu

F32 = jnp.float32
BF16 = jnp.bfloat16

D_MODEL = 1024
HEAD_DIM = 64
RWKV_WIDTH = 1024
DECAY_LORA = 64
AAA_LORA = 64
GATE_LORA = 160
RWKV_COLS = 3 * RWKV_WIDTH + DECAY_LORA + AAA_LORA + GATE_LORA
GN_EPS = 64e-5
ATTN_PAIRS = ((128, 1), (512, 4), (2048, 16))
ATTN_GROUPS = 3
ATTN_GROUP_WIDTH = 256
ATTN_WIDTH = ATTN_GROUPS * ATTN_GROUP_WIDTH
ATTN_BLOCK = 128
D_FF = 2816
RMS_EPS = 1e-6
NEG_INF = -1e30

LANES = 128
MXU_DIM = 256
HEADS_PER_TILE = MXU_DIM // HEAD_DIM
VMEM_LIMIT_BYTES = 56 * 1024 * 1024

LORA_PAD = 128
GATE_LORA_PAD = 256
RW_PAD = 3 * RWKV_WIDTH + LORA_PAD + GATE_LORA_PAD
ATT0 = RW_PAD
GATE0 = ATT0 + 3 * ATTN_WIDTH
IN_PAD = GATE0 + 2 * D_MODEL

FF_CHUNK = D_FF // 2
FFN_TILE = 512
PROJ_TILE = 256
MERGE_TILE = 256
WKV_CHUNK = 64
WKV_LOG2_CHUNK = 6
WKV_SEQS = 1


def _dot(a, b):
    return jnp.dot(a, b, preferred_element_type=F32)


def _dot_nt(a, b):
    return lax.dot_general(a, b, (((1,), (1,)), ((), ())), preferred_element_type=F32)


def _dot_tn(a, b):
    return lax.dot_general(a, b, (((0,), (0,)), ((), ())), preferred_element_type=F32)


def _rms_norm_bf16(x, gain):
    ms = jnp.mean(x * x, axis=-1, keepdims=True)
    return (x * lax.rsqrt(ms + RMS_EPS) * gain).astype(BF16)


def _head_sums(t, ones_bd):
    parts = []
    for q in range(t.shape[-1] // MXU_DIM):
        parts.append(_dot(t[:, q * MXU_DIM:(q + 1) * MXU_DIM].astype(BF16), ones_bd))
    return jnp.concatenate(parts, axis=-1)


def _resident(shape):
    return pl.BlockSpec(shape, lambda *_: (0,) * len(shape), pipeline_mode=pl.Buffered(1))


def _ffn_body(x_ref, gain_ref, w_in_ref, w_out_ref, o_ref):
    x = x_ref[...]
    h = _rms_norm_bf16(x, gain_ref[...])
    acc = None
    for c in range(D_FF // FF_CHUNK):
        lo = c * FF_CHUNK
        gate = _dot(h, w_in_ref[:, lo:lo + FF_CHUNK])
        up = _dot(h, w_in_ref[:, D_FF + lo:D_FF + lo + FF_CHUNK])
        act = (gate * jax.nn.sigmoid(gate) * up).astype(BF16)
        part = _dot(act, w_out_ref[lo:lo + FF_CHUNK, :])
        acc = part if acc is None else acc + part
    o_ref[...] = x + 0.5 * acc


def _ffn(x, gain, w_in, w_out):
    t = x.shape[0]
    tm = FFN_TILE
    return pl.pallas_call(
        _ffn_body,
        grid=(t // tm,),
        in_specs=[
            pl.BlockSpec((tm, D_MODEL), lambda i: (i, 0)),
            _resident((1, D_MODEL)),
            _resident((D_MODEL, 2 * D_FF)),
            _resident((D_FF, D_MODEL)),
        ],
        out_specs=pl.BlockSpec((tm, D_MODEL), lambda i: (i, 0)),
        out_shape=jax.ShapeDtypeStruct((t, D_MODEL), F32),
        compiler_params=pltpu.CompilerParams(
            dimension_semantics=("parallel",), vmem_limit_bytes=VMEM_LIMIT_BYTES),
        name="ffn",
    )(x, gain, w_in, w_out)


def _softplus(z):
    return jnp.maximum(z, 0.0) + jnp.log(1.0 + jnp.exp(-jnp.abs(z)))


def _proj_body(tiles_per_seq,
               x_ref, gain_ref, w_ref, bg_ref, mu_ref, w0_ref, w2_ref, a0_ref, a2_ref, g2_ref,
               kk_ref, ka_ref, rk_ref, qg_ref, kg_ref, ones_ref,
               r_o, lw_o, k_o, v_o, kkn_o, b_o, bonus_o, g_o, aq_o, ak_o, av_o, gate_o,
               carry_ref):
    i = pl.program_id(0)
    tm = x_ref.shape[0]
    w = RWKV_WIDTH
    h = _rms_norm_bf16(x_ref[...], gain_ref[...])
    ones_bd = ones_ref[...]

    p = _dot(h, w_ref[:, :RW_PAD])
    @pl.when((i % tiles_per_seq) == 0)
    def _():
        carry_ref[...] = jnp.zeros_like(carry_ref)

    row = lax.broadcasted_iota(jnp.int32, (tm, 1), 0)
    prev = jnp.where(row == 0, carry_ref[...], pltpu.roll(p, 1, 0))
    carry_ref[...] = p[tm - 1:tm, :]
    ps = p + (prev - p) * mu_ref[...]

    r = ps[:, 0:w]
    k0 = ps[:, w:2 * w]
    v = ps[:, 2 * w:3 * w]
    lora = ps[:, 3 * w:3 * w + LORA_PAD]
    gd = ps[:, 3 * w + LORA_PAD:RW_PAD]

    wraw = w0_ref[...] + _dot(jnp.tanh(lora).astype(BF16), w2_ref[...])
    lw = -jnp.exp(-_softplus(-wraw) - 0.5)
    a = jax.nn.sigmoid(a0_ref[...] + _dot(lora.astype(BF16), a2_ref[...]))
    g = _dot(jax.nn.sigmoid(gd).astype(BF16), g2_ref[...])

    kk = k0 * kk_ref[...]
    kkn = kk * lax.rsqrt(jnp.maximum(_head_sums(kk * kk, ones_bd), 1e-24))
    k = k0 * (1.0 + (a - 1.0) * ka_ref[...])
    bonus = _head_sums(r * k * rk_ref[...], ones_bd) * v

    r_o[...] = r
    lw_o[...] = lw
    k_o[...] = k
    v_o[...] = v
    kkn_o[...] = kkn
    b_o[...] = kkn * a
    bonus_o[...] = bonus
    g_o[...] = g

    pa = _dot(h, w_ref[:, ATT0:GATE0])
    pq = pa[:, 0:ATTN_WIDTH]
    pk = pa[:, ATTN_WIDTH:2 * ATTN_WIDTH]
    inv_n = 1.0 / HEAD_DIM
    qn = pq * lax.rsqrt(_head_sums(pq * pq, ones_bd) * inv_n + RMS_EPS) * qg_ref[...]
    kn = pk * lax.rsqrt(_head_sums(pk * pk, ones_bd) * inv_n + RMS_EPS) * kg_ref[...]
    aq_o[...] = qn.astype(BF16)
    ak_o[...] = kn.astype(BF16)
    av_o[...] = pa[:, 2 * ATTN_WIDTH:].astype(BF16)

    gate_o[...] = jax.nn.sigmoid(_dot(h, w_ref[:, GATE0:]) + bg_ref[...])


def _proj(x, seq_len, gain, w, bg, mu, w0, w2, a0, a2, g2, k_k, k_a, r_k, qg, kg, ones_bd):
    t = x.shape[0]
    tm = PROJ_TILE
    row = lambda n: pl.BlockSpec((tm, n), lambda i: (i, 0))
    wide = jax.ShapeDtypeStruct((t, RWKV_WIDTH), F32)
    att = jax.ShapeDtypeStruct((t, ATTN_WIDTH), BF16)
    return pl.pallas_call(
        functools.partial(_proj_body, seq_len // tm),
        grid=(t // tm,),
        in_specs=[row(D_MODEL)] + [_resident(a.shape) for a in
                                   (gain, w, bg, mu, w0, w2, a0, a2, g2, k_k, k_a, r_k, qg, kg, ones_bd)],
        out_specs=[row(RWKV_WIDTH)] * 8 + [row(ATTN_WIDTH)] * 3 + [row(2 * D_MODEL)],
        out_shape=[wide] * 8 + [att] * 3 + [jax.ShapeDtypeStruct((t, 2 * D_MODEL), F32)],
        scratch_shapes=[pltpu.VMEM((1, RW_PAD), F32)],
        compiler_params=pltpu.CompilerParams(
            dimension_semantics=("arbitrary",), vmem_limit_bytes=VMEM_LIMIT_BYTES),
        name="proj",
    )(x, gain, w, bg, mu, w0, w2, a0, a2, g2, k_k, k_a, r_k, qg, kg, ones_bd)


def _wkv_body(r_ref, lw_ref, k_ref, v_ref, kkn_ref, b_ref, y_ref, z_ref):
    c_len = WKV_CHUNK
    n_seq = r_ref.shape[0]
    n_tiles = RWKV_WIDTH // MXU_DIM

    @pl.when(pl.program_id(1) == 0)
    def _():
        z_ref[...] = jnp.zeros_like(z_ref)

    ri = lax.broadcasted_iota(jnp.int32, (c_len, c_len), 0)
    ci = lax.broadcasted_iota(jnp.int32, (c_len, c_len), 1)
    tri = jnp.where(ci <= ri, 1.0, 0.0).astype(BF16)
    lane = lax.broadcasted_iota(jnp.int32, (c_len, MXU_DIM), 1)
    rowc = lax.broadcasted_iota(jnp.int32, (c_len, MXU_DIM), 0)
    head_mask = [(lane // HEAD_DIM) == h for h in range(HEADS_PER_TILE)]
    blk_mask = [(lane // c_len) == h for h in range(HEADS_PER_TILE)]
    col_in_blk = lane % c_len
    strict = col_in_blk < rowc
    incl = col_in_blk <= rowc
    eye = jnp.where(col_in_blk == rowc, 1.0, 0.0)
    br = lax.broadcasted_iota(jnp.int32, (MXU_DIM, MXU_DIM), 0) // HEAD_DIM
    bc = lax.broadcasted_iota(jnp.int32, (MXU_DIM, MXU_DIM), 1) // HEAD_DIM
    bd_mask = br == bc

    def stack(x, masks):
        return jnp.concatenate([jnp.where(m, x, 0.0) for m in masks], axis=0).astype(BF16)

    def rows(*xs):
        return jnp.concatenate(xs, axis=0)

    units = []
    for s in range(n_seq):
        lw = lw_ref[s]
        hi = lw.astype(BF16)
        rem = lw - hi.astype(F32)
        mid = rem.astype(BF16)
        lo = (rem - mid.astype(F32)).astype(BF16)
        cum = _dot(tri, hi) + _dot(tri, mid) + _dot(tri, lo)
        ref = cum[c_len // 2 - 1:c_len // 2, :]
        e_fwd = jnp.exp(cum - ref)
        e_prev = jnp.exp(cum - lw - ref)
        e_bwd = jnp.exp(ref - cum)
        e_ref = jnp.exp(ref)
        lam_end = e_fwd[c_len - 1:c_len, :]
        lam_c = jnp.exp(cum[c_len - 1:c_len, :])
        at = -kkn_ref[s] * e_prev
        rt = r_ref[s] * e_fwd
        kt = k_ref[s] * e_bwd
        bt = b_ref[s] * e_bwd
        v = v_ref[s]
        for q in range(n_tiles):
            sl = slice(q * MXU_DIM, (q + 1) * MXU_DIM)
            units.append(dict(
                s=s, sl=sl, zi=s * n_tiles + q, at=at[:, sl], rt=rt[:, sl], kt=kt[:, sl],
                bt=bt[:, sl], v=v[:, sl], e_ref=e_ref[:, sl], lam_end=lam_end[:, sl],
                lam_c=lam_c[:, sl]))

    for u in units:
        ar = rows(u["at"], u["rt"]).astype(BF16)
        u["sb"] = _dot_nt(ar, stack(u["bt"], head_mask))
        u["sk"] = _dot_nt(ar, stack(u["kt"], head_mask))
    for u in units:
        u["a_ab"] = jnp.where(strict, u["sb"][:c_len], 0.0)
        u["a_rb"] = jnp.where(incl, u["sb"][c_len:], 0.0).astype(BF16)
        a_ak = jnp.where(strict, u["sk"][:c_len], 0.0)
        a_rk = jnp.where(incl, u["sk"][c_len:], 0.0)
        u["xv"] = _dot(rows(a_ak, a_rk).astype(BF16), stack(u["v"], head_mask))
        u["pw"] = _dot(u["a_ab"].astype(BF16), stack(u["a_ab"], blk_mask))
        u["t_inv"] = eye + u["a_ab"]
    for lev in range(1, WKV_LOG2_CHUNK):
        for u in units:
            rhs = stack(u["pw"], blk_mask)
            if lev < WKV_LOG2_CHUNK - 1:
                prod = _dot(rows(u["pw"], u["t_inv"]).astype(BF16), rhs)
                u["pw"] = prod[:c_len]
                u["t_inv"] = u["t_inv"] + prod[c_len:]
            else:
                u["t_inv"] = u["t_inv"] + _dot(u["t_inv"].astype(BF16), rhs)
    for u in units:
        t16 = u["t_inv"].astype(BF16)
        u["wt"] = _dot(t16, stack(u["at"] * u["e_ref"], head_mask))
        u["u0"] = _dot(t16, stack(u["xv"][:c_len], head_mask))
    for u in units:
        u["qh"] = u["rt"] * u["e_ref"] + _dot(u["a_rb"], stack(u["wt"], head_mask))
        u["y0"] = _dot(u["a_rb"], stack(u["u0"], head_mask)) + u["xv"][c_len:]
        bh = (u["bt"] * u["lam_end"]).astype(BF16)
        kh = (u["kt"] * u["lam_end"]).astype(BF16)
        u["m_lr"] = _dot_tn(u["wt"].astype(BF16), bh)
        u["g"] = _dot_tn(rows(u["u0"], u["v"]).astype(BF16), rows(bh, kh))
    for u in units:
        z = z_ref[u["zi"]]
        z16 = z.astype(BF16)
        y_ref[u["s"], :, u["sl"]] = _dot_nt(u["qh"].astype(BF16), z16) + u["y0"]
        z_ref[u["zi"]] = (z * u["lam_c"]
                          + _dot(z16, jnp.where(bd_mask, u["m_lr"], 0.0).astype(BF16))
                          + jnp.where(bd_mask, u["g"], 0.0))


def _wkv(batch, seq_len, r, lw, k, v, kkn, b):
    n_chunks = seq_len // WKV_CHUNK
    n_seq = WKV_SEQS
    view = lambda a: a.reshape(batch, seq_len, RWKV_WIDTH)
    blk = pl.BlockSpec((n_seq, WKV_CHUNK, RWKV_WIDTH), lambda bi, ci: (bi, ci, 0))
    y = pl.pallas_call(
        _wkv_body,
        grid=(batch // n_seq, n_chunks),
        in_specs=[blk] * 6,
        out_specs=blk,
        out_shape=jax.ShapeDtypeStruct((batch, seq_len, RWKV_WIDTH), F32),
        scratch_shapes=[pltpu.VMEM((n_seq * RWKV_WIDTH // MXU_DIM, MXU_DIM, MXU_DIM), F32)],
        compiler_params=pltpu.CompilerParams(
            dimension_semantics=("arbitrary", "arbitrary"), vmem_limit_bytes=VMEM_LIMIT_BYTES),
        name="wkv",
    )(*(view(a) for a in (r, lw, k, v, kkn, b)))
    return y.reshape(batch * seq_len, RWKV_WIDTH)


def _attn_body(blocks_per_sub, *refs):
    n_in = 5 * ATTN_GROUPS
    in_refs, out_refs = refs[:n_in], refs[n_in:]
    i = pl.program_id(1)
    blk = ATTN_BLOCK
    qi = lax.broadcasted_iota(jnp.int32, (HEADS_PER_TILE * blk, 2 * blk), 0) % blk
    kj = lax.broadcasted_iota(jnp.int32, (HEADS_PER_TILE * blk, 2 * blk), 1)
    dist = qi + blk - kj
    window = (dist >= 0) & (dist <= blk)
    lane = lax.broadcasted_iota(jnp.int32, (blk, ATTN_GROUP_WIDTH), 1)
    head_mask = [(lane // HEAD_DIM) == h for h in range(HEADS_PER_TILE)]

    for g in range(ATTN_GROUPS):
        q_ref, kp_ref, kc_ref, vp_ref, vc_ref = in_refs[5 * g:5 * g + 5]
        o_ref, l_ref = out_refs[2 * g:2 * g + 2]
        has_prev = (i % blocks_per_sub[g]) > 0
        valid = window & (has_prev | (kj >= blk))
        q = q_ref[...]
        qs = jnp.concatenate([jnp.where(m, q, jnp.zeros_like(q)) for m in head_mask], axis=0)
        kcat = jnp.concatenate([kp_ref[...], kc_ref[...]], axis=0)
        vcat = jnp.concatenate([vp_ref[...], vc_ref[...]], axis=0)
        s = jnp.where(valid, _dot_nt(qs, kcat), NEG_INF)
        m = jnp.max(s, axis=-1, keepdims=True)
        p = jnp.exp(s - m)
        den = jnp.sum(p, axis=-1, keepdims=True)
        o_all = _dot(p.astype(BF16), vcat) / den
        lse = m + jnp.log(den)
        o = jnp.zeros((blk, ATTN_GROUP_WIDTH), F32)
        l = jnp.zeros((blk, ATTN_GROUP_WIDTH), F32)
        for h in range(HEADS_PER_TILE):
            rows = slice(h * blk, (h + 1) * blk)
            o = jnp.where(head_mask[h], o_all[rows], o)
            l = jnp.where(head_mask[h], lse[rows], l)
        o_ref[...] = o
        l_ref[...] = l


def _attn(batch, seq_len, aq, ak, av):
    blk = ATTN_BLOCK
    n_steps = seq_len // blk
    in_specs, args, out_specs, out_shape, blocks_per_sub = [], [], [], [], []
    for g, (_, dil) in enumerate(ATTN_PAIRS):
        sub = seq_len // dil
        nb = sub // blk
        blocks_per_sub.append(nb)

        def cur(bi, i, g=g, nb=nb):
            return (bi, i % nb, (i // nb) * ATTN_GROUPS + g)

        def prev(bi, i, g=g, nb=nb):
            return (bi, jnp.maximum(i % nb - 1, 0), (i // nb) * ATTN_GROUPS + g)

        def out(bi, i, nb=nb):
            return (bi, i % nb, i // nb)

        view = lambda a: a.reshape(batch, sub, dil * ATTN_WIDTH)
        spec = lambda f: pl.BlockSpec((None, blk, ATTN_GROUP_WIDTH), f)
        in_specs += [spec(cur), spec(prev), spec(cur), spec(prev), spec(cur)]
        args += [view(aq), view(ak), view(ak), view(av), view(av)]
        out_specs += [spec(out), spec(out)]
        out_shape += [jax.ShapeDtypeStruct((batch, sub, dil * ATTN_GROUP_WIDTH), F32)] * 2
    outs = pl.pallas_call(
        functools.partial(_attn_body, tuple(blocks_per_sub)),
        grid=(batch, n_steps),
        in_specs=in_specs,
        out_specs=out_specs,
        out_shape=out_shape,
        compiler_params=pltpu.CompilerParams(
            dimension_semantics=("parallel", "parallel"), vmem_limit_bytes=VMEM_LIMIT_BYTES),
        name="attn",
    )(*args)
    return [o.reshape(batch * seq_len, ATTN_GROUP_WIDTH) for o in outs]


def _merge_body(x_ref, wkv_ref, bonus_ref, g_ref, gate_ref,
                o0_ref, l0_ref, o1_ref, l1_ref, o2_ref, l2_ref,
                lnw_ref, lnb_ref, ones_ref, wpr_ref, wpa_ref, wo_ref, out_ref):
    ones_bd = ones_ref[...]
    inv_n = 1.0 / HEAD_DIM
    wkv = wkv_ref[...]
    dev = wkv - _head_sums(wkv, ones_bd) * inv_n
    var = _head_sums(dev * dev, ones_bd) * inv_n
    y = dev * lax.rsqrt(var + GN_EPS) * lnw_ref[...] + lnb_ref[...]
    y_a = ((y + bonus_ref[...]) * g_ref[...]).astype(BF16)
    pa = _dot(y_a, wpr_ref[...])
    l0, l1, l2 = l0_ref[...], l1_ref[...], l2_ref[...]
    mx = jnp.maximum(jnp.maximum(l0, l1), l2)
    e0, e1, e2 = jnp.exp(l0 - mx), jnp.exp(l1 - mx), jnp.exp(l2 - mx)
    inv = 1.0 / (e0 + e1 + e2)
    pb = None
    for gi, (o_ref, e) in enumerate(((o0_ref, e0), (o1_ref, e1), (o2_ref, e2))):
        yb = (o_ref[...] * (e * inv)).astype(BF16)
        part = _dot(yb, wpa_ref[gi * ATTN_GROUP_WIDTH:(gi + 1) * ATTN_GROUP_WIDTH, :])
        pb = part if pb is None else pb + part
    gates = gate_ref[...]
    merged = gates[:, :D_MODEL] * pa + gates[:, D_MODEL:] * pb
    out_ref[...] = x_ref[...] + _dot(merged.astype(BF16), wo_ref[...])


def _merge(x, wkv, bonus, g, gates, attn_outs, ln_w, ln_b, ones_bd, wpr, wpa, wo):
    t = x.shape[0]
    tm = MERGE_TILE
    row = lambda n: pl.BlockSpec((tm, n), lambda i: (i, 0))
    small = (ln_w, ln_b, ones_bd, wpr, wpa, wo)
    return pl.pallas_call(
        _merge_body,
        grid=(t // tm,),
        in_specs=[row(D_MODEL)] * 4 + [row(2 * D_MODEL)] + [row(ATTN_GROUP_WIDTH)] * 6
                 + [_resident(a.shape) for a in small],
        out_specs=row(D_MODEL),
        out_shape=jax.ShapeDtypeStruct((t, D_MODEL), F32),
        compiler_params=pltpu.CompilerParams(
            dimension_semantics=("parallel",), vmem_limit_bytes=VMEM_LIMIT_BYTES),
        name="merge",
    )(x, wkv, bonus, g, gates, *attn_outs, *small)


def _pad_rows(a, n):
    return jnp.pad(a, ((0, n - a.shape[0]), (0, 0)))


def _layer(x, seq_len, ffn1_norm, ffn1_w_in, ffn1_w_out, mix_norm, w_in, b_gate, rwkv_mu,
           rwkv_w0, rwkv_w2, rwkv_a0, rwkv_a2, rwkv_g2, rwkv_k_k, rwkv_k_a, rwkv_r_k,
           rwkv_ln_w, rwkv_ln_b, attn_q_norm, attn_k_norm, w_proj_rwkv, w_proj_attn,
           w_out, ffn2_norm, ffn2_w_in, ffn2_w_out):
    batch = x.shape[0] // seq_len
    row = lambda a: a.reshape(1, -1)
    w3 = 3 * RWKV_WIDTH
    lora_cols = DECAY_LORA + AAA_LORA

    zeros = lambda n: jnp.zeros((D_MODEL, n), w_in.dtype)
    w_pad = jnp.concatenate([
        w_in[:, :w3 + lora_cols + GATE_LORA], zeros(GATE_LORA_PAD - GATE_LORA),
        w_in[:, RWKV_COLS:]], axis=1).astype(BF16)
    mu_pad = jnp.pad(rwkv_mu, (0, RW_PAD - RWKV_COLS)).reshape(1, RW_PAD)
    w2_pad = _pad_rows(rwkv_w2, LORA_PAD).astype(BF16)
    a2_pad = jnp.pad(rwkv_a2, ((DECAY_LORA, 0), (0, 0))).astype(BF16)
    g2_pad = _pad_rows(rwkv_g2, GATE_LORA_PAD).astype(BF16)
    n_heads = ATTN_WIDTH // HEAD_DIM
    q_gain = row(jnp.tile(attn_q_norm, n_heads) * HEAD_DIM ** -0.5)
    k_gain = row(jnp.tile(attn_k_norm, n_heads))
    idx = jnp.arange(MXU_DIM) // HEAD_DIM
    ones_bd = (idx[:, None] == idx[None, :]).astype(BF16)

    x = _ffn(x, row(ffn1_norm), ffn1_w_in.astype(BF16), ffn1_w_out.astype(BF16))
    (r, lw, k, v, kkn, b, bonus, g, aq, ak, av, gates) = _proj(
        x, seq_len, row(mix_norm), w_pad, row(b_gate), mu_pad, row(rwkv_w0), w2_pad,
        row(rwkv_a0), a2_pad, g2_pad, row(rwkv_k_k), row(rwkv_k_a), row(rwkv_r_k),
        q_gain, k_gain, ones_bd)
    wkv = _wkv(batch, seq_len, r, lw, k, v, kkn, b)
    attn_outs = _attn(batch, seq_len, aq, ak, av)
    x = _merge(x, wkv, bonus, g, gates, attn_outs, row(rwkv_ln_w), row(rwkv_ln_b), ones_bd,
               w_proj_rwkv.astype(BF16), w_proj_attn.astype(BF16), w_out.astype(BF16))
    return _ffn(x, row(ffn2_norm), ffn2_w_in.astype(BF16), ffn2_w_out.astype(BF16))


def kernel(x, ffn1_norm, ffn1_w_in, ffn1_w_out, mix_norm, w_in, b_gate, rwkv_mu, rwkv_w0, rwkv_w2, rwkv_a0, rwkv_a2, rwkv_g2, rwkv_k_k, rwkv_k_a, rwkv_r_k, rwkv_ln_w, rwkv_ln_b, attn_q_norm, attn_k_norm, w_proj_rwkv, w_proj_attn, w_out, ffn2_norm, ffn2_w_in, ffn2_w_out):
    batch, seq_len, d = x.shape
    params = (ffn1_norm, ffn1_w_in, ffn1_w_out, mix_norm, w_in, b_gate, rwkv_mu, rwkv_w0, rwkv_w2,
              rwkv_a0, rwkv_a2, rwkv_g2, rwkv_k_k, rwkv_k_a, rwkv_r_k, rwkv_ln_w, rwkv_ln_b,
              attn_q_norm, attn_k_norm, w_proj_rwkv, w_proj_attn, w_out, ffn2_norm, ffn2_w_in,
              ffn2_w_out)
    h = x.reshape(batch * seq_len, d)
    for layer in range(ffn1_norm.shape[0]):
        h = _layer(h, seq_len, *(p[layer] for p in params))
    return h.reshape(batch, seq_len, d)
```

```python
import functools

import jax
import jax.numpy as jnp
from jax import lax
from jax.experimental import pallas as pl
from jax.experimental.pallas import tpu as pltpu

F32 = jnp.float32
BF16 = jnp.bfloat16

D_MODEL = 1024
HEAD_DIM = 64
RWKV_WIDTH = 1024
DECAY_LORA = 64
AAA_LORA = 64
GATE_LORA = 160
RWKV_COLS = 3 * RWKV_WIDTH + DECAY_LORA + AAA_LORA + GATE_LORA
GN_EPS = 64e-5
ATTN_PAIRS = ((128, 1), (512, 4), (2048, 16))
ATTN_GROUPS = 3
ATTN_GROUP_WIDTH = 256
ATTN_WIDTH = ATTN_GROUPS * ATTN_GROUP_WIDTH
ATTN_BLOCK = 128
D_FF = 2816
RMS_EPS = 1e-6
NEG_INF = -1e30

LANES = 128
MXU_DIM = 256
HEADS_PER_TILE = MXU_DIM // HEAD_DIM
VMEM_LIMIT_BYTES = 56 * 1024 * 1024

LORA_PAD = 128
GATE_LORA_PAD = 256
RW_PAD = 3 * RWKV_WIDTH + LORA_PAD + GATE_LORA_PAD
ATT0 = RW_PAD
GATE0 = ATT0 + 3 * ATTN_WIDTH
IN_PAD = GATE0 + 2 * D_MODEL

FF_CHUNK = D_FF // 2
FFN_TILE = 512
PROJ_TILE = 256
MERGE_TILE = 256
WKV_CHUNK = 64
WKV_LOG2_CHUNK = 6
WKV_SEQS = 2


def _dot(a, b):
    return jnp.dot(a, b, preferred_element_type=F32)


def _dot_nt(a, b):
    return lax.dot_general(a, b, (((1,), (1,)), ((), ())), preferred_element_type=F32)


def _dot_tn(a, b):
    return lax.dot_general(a, b, (((0,), (0,)), ((), ())), preferred_element_type=F32)


def _rms_norm_bf16(x, gain):
    ms = jnp.mean(x * x, axis=-1, keepdims=True)
    return (x * lax.rsqrt(ms + RMS_EPS) * gain).astype(BF16)


def _head_sums(t, ones_bd):
    parts = []
    for q in range(t.shape[-1] // MXU_DIM):
        parts.append(_dot(t[:, q * MXU_DIM:(q + 1) * MXU_DIM].astype(BF16), ones_bd))
    return jnp.concatenate(parts, axis=-1)


def _resident(shape):
    return pl.BlockSpec(shape, lambda *_: (0,) * len(shape), pipeline_mode=pl.Buffered(1))


def _ffn_body(x_ref, gain_ref, w_in_ref, w_out_ref, o_ref):
    x = x_ref[...]
    h = _rms_norm_bf16(x, gain_ref[...])
    acc = None
    for c in range(D_FF // FF_CHUNK):
        lo = c * FF_CHUNK
        gate = _dot(h, w_in_ref[:, lo:lo + FF_CHUNK])
        up = _dot(h, w_in_ref[:, D_FF + lo:D_FF + lo + FF_CHUNK])
        act = (gate * jax.nn.sigmoid(gate) * up).astype(BF16)
        part = _dot(act, w_out_ref[lo:lo + FF_CHUNK, :])
        acc = part if acc is None else acc + part
    o_ref[...] = x + 0.5 * acc


def _ffn(x, gain, w_in, w_out):
    t = x.shape[0]
    tm = FFN_TILE
    return pl.pallas_call(
        _ffn_body,
        grid=(t // tm,),
        in_specs=[
            pl.BlockSpec((tm, D_MODEL), lambda i: (i, 0)),
            _resident((1, D_MODEL)),
            _resident((D_MODEL, 2 * D_FF)),
            _resident((D_FF, D_MODEL)),
        ],
        out_specs=pl.BlockSpec((tm, D_MODEL), lambda i: (i, 0)),
        out_shape=jax.ShapeDtypeStruct((t, D_MODEL), F32),
        compiler_params=pltpu.CompilerParams(
            dimension_semantics=("parallel",), vmem_limit_bytes=VMEM_LIMIT_BYTES),
        name="ffn",
    )(x, gain, w_in, w_out)


def _softplus(z):
    return jnp.maximum(z, 0.0) + jnp.log(1.0 + jnp.exp(-jnp.abs(z)))


def _proj_body(tiles_per_seq,
               x_ref, gain_ref, w_ref, bg_ref, mu_ref, w0_ref, w2_ref, a0_ref, a2_ref, g2_ref,
               kk_ref, ka_ref, rk_ref, qg_ref, kg_ref, ones_ref,
               r_o, lw_o, k_o, v_o, kkn_o, b_o, bonus_o, g_o, aq_o, ak_o, av_o, gate_o,
               carry_ref):
    i = pl.program_id(0)
    tm = x_ref.shape[0]
    w = RWKV_WIDTH
    h = _rms_norm_bf16(x_ref[...], gain_ref[...])
    ones_bd = ones_ref[...]
    row = lax.broadcasted_iota(jnp.int32, (tm, 1), 0)

    @pl.when((i % tiles_per_seq) == 0)
    def _():
        carry_ref[...] = jnp.zeros_like(carry_ref)

    def mm(lo, hi):
        return _dot(h, w_ref[:, lo:hi])

    def shift(p, lo, hi):
        prev = jnp.where(row == 0, carry_ref[:, lo:hi], pltpu.roll(p, 1, 0))
        carry_ref[:, lo:hi] = p[tm - 1:tm, :]
        return p + (prev - p) * mu_ref[:, lo:hi]

    p_l = mm(3 * w, RW_PAD)
    p_k = mm(w, 2 * w)

    ps_l = shift(p_l, 3 * w, RW_PAD)
    lora = ps_l[:, :LORA_PAD]
    gd = ps_l[:, LORA_PAD:]
    wraw = w0_ref[...] + _dot(jnp.tanh(lora).astype(BF16), w2_ref[...])
    lw_o[...] = -jnp.exp(-_softplus(-wraw) - 0.5)
    a = jax.nn.sigmoid(a0_ref[...] + _dot(lora.astype(BF16), a2_ref[...]))
    g_o[...] = _dot(jax.nn.sigmoid(gd).astype(BF16), g2_ref[...]).astype(BF16)

    p_r = mm(0, w)

    k0 = shift(p_k, w, 2 * w)
    kk = k0 * kk_ref[...]
    kkn = kk * lax.rsqrt(jnp.maximum(_head_sums(kk * kk, ones_bd), 1e-24))
    k = k0 * (1.0 + (a - 1.0) * ka_ref[...])
    k_o[...] = k.astype(BF16)
    kkn_o[...] = kkn.astype(BF16)
    b_o[...] = (kkn * a).astype(BF16)

    p_v = mm(2 * w, 3 * w)

    r = shift(p_r, 0, w)
    r_o[...] = r.astype(BF16)
    rk_sum = _head_sums(r * k * rk_ref[...], ones_bd)

    pa_qk = mm(ATT0, ATT0 + 2 * ATTN_WIDTH)

    v = shift(p_v, 2 * w, 3 * w)
    v_o[...] = v.astype(BF16)
    bonus_o[...] = (rk_sum * v).astype(BF16)

    pa_v = mm(ATT0 + 2 * ATTN_WIDTH, GATE0)
    pg_a = mm(GATE0, GATE0 + D_MODEL)

    pq = pa_qk[:, 0:ATTN_WIDTH]
    pk = pa_qk[:, ATTN_WIDTH:]
    inv_n = 1.0 / HEAD_DIM
    qn = pq * lax.rsqrt(_head_sums(pq * pq, ones_bd) * inv_n + RMS_EPS) * qg_ref[...]
    kn = pk * lax.rsqrt(_head_sums(pk * pk, ones_bd) * inv_n + RMS_EPS) * kg_ref[...]
    aq_o[...] = qn.astype(BF16)
    ak_o[...] = kn.astype(BF16)
    av_o[...] = pa_v.astype(BF16)

    pg_b = mm(GATE0 + D_MODEL, IN_PAD)
    gate_o[:, :D_MODEL] = jax.nn.sigmoid(pg_a + bg_ref[:, :D_MODEL]).astype(BF16)
    gate_o[:, D_MODEL:] = jax.nn.sigmoid(pg_b + bg_ref[:, D_MODEL:]).astype(BF16)


def _proj(x, seq_len, gain, w, bg, mu, w0, w2, a0, a2, g2, k_k, k_a, r_k, qg, kg, ones_bd):
    t = x.shape[0]
    tm = PROJ_TILE
    row = lambda n: pl.BlockSpec((tm, n), lambda i: (i, 0))
    wide = jax.ShapeDtypeStruct((t, RWKV_WIDTH), BF16)
    wide32 = jax.ShapeDtypeStruct((t, RWKV_WIDTH), F32)
    att = jax.ShapeDtypeStruct((t, ATTN_WIDTH), BF16)
    return pl.pallas_call(
        functools.partial(_proj_body, seq_len // tm),
        grid=(t // tm,),
        in_specs=[row(D_MODEL)] + [_resident(a.shape) for a in
                                   (gain, w, bg, mu, w0, w2, a0, a2, g2, k_k, k_a, r_k, qg, kg, ones_bd)],
        out_specs=[row(RWKV_WIDTH)] * 8 + [row(ATTN_WIDTH)] * 3 + [row(2 * D_MODEL)],
        out_shape=[wide, wide32] + [wide] * 6 + [att] * 3
                  + [jax.ShapeDtypeStruct((t, 2 * D_MODEL), BF16)],
        scratch_shapes=[pltpu.VMEM((1, RW_PAD), F32)],
        compiler_params=pltpu.CompilerParams(
            dimension_semantics=("arbitrary",), vmem_limit_bytes=VMEM_LIMIT_BYTES),
        name="proj",
    )(x, gain, w, bg, mu, w0, w2, a0, a2, g2, k_k, k_a, r_k, qg, kg, ones_bd)


def _wkv_body(r_ref, lw_ref, k_ref, v_ref, kkn_ref, b_ref, y_ref, z_ref):
    c_len = WKV_CHUNK
    n_seq = r_ref.shape[0]
    n_tiles = RWKV_WIDTH // MXU_DIM

    @pl.when(pl.program_id(1) == 0)
    def _():
        z_ref[...] = jnp.zeros_like(z_ref)

    ri = lax.broadcasted_iota(jnp.int32, (c_len, c_len), 0)
    ci = lax.broadcasted_iota(jnp.int32, (c_len, c_len), 1)
    tri = jnp.where(ci <= ri, 1.0, 0.0).astype(BF16)
    lane = lax.broadcasted_iota(jnp.int32, (c_len, MXU_DIM), 1)
    rowc = lax.broadcasted_iota(jnp.int32, (c_len, MXU_DIM), 0)
    head_mask = [(lane // HEAD_DIM) == h for h in range(HEADS_PER_TILE)]
    blk_mask = [(lane // c_len) == h for h in range(HEADS_PER_TILE)]
    col_in_blk = lane % c_len
    strict = col_in_blk < rowc
    incl = col_in_blk <= rowc
    eye = jnp.where(col_in_blk == rowc, 1.0, 0.0)
    br = lax.broadcasted_iota(jnp.int32, (MXU_DIM, MXU_DIM), 0) // HEAD_DIM
    bc = lax.broadcasted_iota(jnp.int32, (MXU_DIM, MXU_DIM), 1) // HEAD_DIM
    bd_mask = br == bc

    def stack(x, masks):
        return jnp.concatenate([jnp.where(m, x, 0.0) for m in masks], axis=0).astype(BF16)

    def rows(*xs):
        return jnp.concatenate(xs, axis=0)

    units = []
    for s in range(n_seq):
        lw = lw_ref[s]
        hi = lw.astype(BF16)
        rem = lw - hi.astype(F32)
        mid = rem.astype(BF16)
        lo = (rem - mid.astype(F32)).astype(BF16)
        cum = _dot(tri, hi) + _dot(tri, mid) + _dot(tri, lo)
        ref = cum[c_len // 2 - 1:c_len // 2, :]
        e_fwd = jnp.exp(cum - ref)
        e_prev = jnp.exp(cum - lw - ref)
        e_bwd = jnp.exp(ref - cum)
        e_ref = jnp.exp(ref)
        lam_end = e_fwd[c_len - 1:c_len, :]
        lam_c = jnp.exp(cum[c_len - 1:c_len, :])
        at = -kkn_ref[s].astype(F32) * e_prev
        rt = r_ref[s].astype(F32) * e_fwd
        kt = k_ref[s].astype(F32) * e_bwd
        bt = b_ref[s].astype(F32) * e_bwd
        v = v_ref[s].astype(F32)
        for q in range(n_tiles):
            sl = slice(q * MXU_DIM, (q + 1) * MXU_DIM)
            units.append(dict(
                s=s, sl=sl, zi=s * n_tiles + q, at=at[:, sl], rt=rt[:, sl], kt=kt[:, sl],
                bt=bt[:, sl], v=v[:, sl], e_ref=e_ref[:, sl], lam_end=lam_end[:, sl],
                lam_c=lam_c[:, sl]))

    for u in units:
        ar = rows(u["at"], u["rt"]).astype(BF16)
        u["sb"] = _dot_nt(ar, stack(u["bt"], head_mask))
        u["sk"] = _dot_nt(ar, stack(u["kt"], head_mask))
    for u in units:
        u["a_ab"] = jnp.where(strict, u["sb"][:c_len], 0.0)
        u["a_rb"] = jnp.where(incl, u["sb"][c_len:], 0.0).astype(BF16)
        a_ak = jnp.where(strict, u["sk"][:c_len], 0.0)
        a_rk = jnp.where(incl, u["sk"][c_len:], 0.0)
        u["xv"] = _dot(rows(a_ak, a_rk).astype(BF16), stack(u["v"], head_mask))
        u["pw"] = _dot(u["a_ab"].astype(BF16), stack(u["a_ab"], blk_mask))
        u["t_inv"] = eye + u["a_ab"]
    for lev in range(1, WKV_LOG2_CHUNK):
        for u in units:
            rhs = stack(u["pw"], blk_mask)
            if lev < WKV_LOG2_CHUNK - 1:
                prod = _dot(rows(u["pw"], u["t_inv"]).astype(BF16), rhs)
                u["pw"] = prod[:c_len]
                u["t_inv"] = u["t_inv"] + prod[c_len:]
            else:
                u["t_inv"] = u["t_inv"] + _dot(u["t_inv"].astype(BF16), rhs)
    for u in units:
        t16 = u["t_inv"].astype(BF16)
        u["wt"] = _dot(t16, stack(u["at"] * u["e_ref"], head_mask))
        u["u0"] = _dot(t16, stack(u["xv"][:c_len], head_mask))
    for u in units:
        u["qh"] = u["rt"] * u["e_ref"] + _dot(u["a_rb"], stack(u["wt"], head_mask))
        u["y0"] = _dot(u["a_rb"], stack(u["u0"], head_mask)) + u["xv"][c_len:]
        bh = (u["bt"] * u["lam_end"]).astype(BF16)
        kh = (u["kt"] * u["lam_end"]).astype(BF16)
        u["m_lr"] = _dot_tn(u["wt"].astype(BF16), bh)
        u["g"] = _dot_tn(rows(u["u0"], u["v"]).astype(BF16), rows(bh, kh))
    for u in units:
        z = z_ref[u["zi"]]
        z16 = z.astype(BF16)
        y_ref[u["s"], :, u["sl"]] = (_dot_nt(u["qh"].astype(BF16), z16) + u["y0"]).astype(BF16)
        z_ref[u["zi"]] = (z * u["lam_c"]
                          + _dot(z16, jnp.where(bd_mask, u["m_lr"], 0.0).astype(BF16))
                          + jnp.where(bd_mask, u["g"], 0.0))


def _wkv(batch, seq_len, r, lw, k, v, kkn, b):
    n_chunks = seq_len // WKV_CHUNK
    n_seq = WKV_SEQS if batch % WKV_SEQS == 0 else 1
    view = lambda a: a.reshape(batch, seq_len, RWKV_WIDTH)
    blk = pl.BlockSpec((n_seq, WKV_CHUNK, RWKV_WIDTH), lambda bi, ci: (bi, ci, 0))
    y = pl.pallas_call(
        _wkv_body,
        grid=(batch // n_seq, n_chunks),
        in_specs=[blk] * 6,
        out_specs=blk,
        out_shape=jax.ShapeDtypeStruct((batch, seq_len, RWKV_WIDTH), BF16),
        scratch_shapes=[pltpu.VMEM((n_seq * RWKV_WIDTH // MXU_DIM, MXU_DIM, MXU_DIM), F32)],
        compiler_params=pltpu.CompilerParams(
            dimension_semantics=("arbitrary", "arbitrary"), vmem_limit_bytes=VMEM_LIMIT_BYTES),
        name="wkv",
    )(*(view(a) for a in (r, lw, k, v, kkn, b)))
    return y.reshape(batch * seq_len, RWKV_WIDTH)


def _attn_body(seq_len, q_ref, k_ref, v_ref, o_ref, l_ref, qsub, ksub, vsub, osub, lsub):
    g = pl.program_id(1)
    blk = ATTN_BLOCK
    n_blocks = seq_len // blk

    def regroup(dil):
        sub = seq_len // dil
        for src, dst, off in ((q_ref, qsub, 0), (k_ref, ksub, blk), (v_ref, vsub, blk)):
            if dil == 1:
                dst[off:off + seq_len, :] = src[...]
            else:
                for half in range(2):
                    lsub[half] = src[:, half * LANES:(half + 1) * LANES].astype(F32)
                for c in range(dil):
                    for r0 in range(0, sub, blk):
                        for half in range(2):
                            rows = slice(off + c * sub + r0, off + c * sub + r0 + blk)
                            dst[rows, half * LANES:(half + 1) * LANES] = (
                                lsub[half, pl.ds(c + r0 * dil, blk, stride=dil), :].astype(BF16))

    def scatter(dil):
        sub = seq_len // dil
        for src, dst in ((osub, o_ref), (lsub, l_ref)):
            for half in range(2):
                if dil == 1:
                    dst[half] = src[half]
                else:
                    for c in range(dil):
                        for r0 in range(0, sub, blk):
                            dst[half, pl.ds(c + r0 * dil, blk, stride=dil), :] = (
                                src[half, c * sub + r0:c * sub + r0 + blk, :])

    ksub[0:blk, :] = jnp.zeros((blk, ATTN_GROUP_WIDTH), BF16)
    vsub[0:blk, :] = jnp.zeros((blk, ATTN_GROUP_WIDTH), BF16)
    for gi, (_, dil) in enumerate(ATTN_PAIRS):
        pl.when(g == gi)(functools.partial(regroup, dil))

    blocks_per_sub = jnp.int32(n_blocks)
    for gi, (_, dil) in enumerate(ATTN_PAIRS):
        blocks_per_sub = jnp.where(g == gi, n_blocks // dil, blocks_per_sub)

    qi = lax.broadcasted_iota(jnp.int32, (HEADS_PER_TILE * blk, 2 * blk), 0) % blk
    kj = lax.broadcasted_iota(jnp.int32, (HEADS_PER_TILE * blk, 2 * blk), 1)
    dist = qi + blk - kj
    window = (dist >= 0) & (dist <= blk)
    lane = lax.broadcasted_iota(jnp.int32, (blk, ATTN_GROUP_WIDTH), 1)
    head_mask = [(lane // HEAD_DIM) == h for h in range(HEADS_PER_TILE)]

    def block(i, carry):
        r0 = pl.multiple_of(i * blk, blk)
        has_prev = (i % blocks_per_sub) > 0
        valid = window & (has_prev | (kj >= blk))
        q = qsub[pl.ds(r0, blk), :]
        qs = jnp.concatenate([jnp.where(m, q, jnp.zeros_like(q)) for m in head_mask], axis=0)
        kcat = ksub[pl.ds(r0, 2 * blk), :]
        vcat = vsub[pl.ds(r0, 2 * blk), :]
        s = jnp.where(valid, _dot_nt(qs, kcat), NEG_INF)
        m = jnp.max(s, axis=-1, keepdims=True)
        p = jnp.exp(s - m)
        den = jnp.sum(p, axis=-1, keepdims=True)
        o_all = _dot(p.astype(BF16), vcat) / den
        lse = m + jnp.log(den)
        o = jnp.zeros((blk, ATTN_GROUP_WIDTH), F32)
        l = jnp.zeros((blk, ATTN_GROUP_WIDTH), F32)
        for h in range(HEADS_PER_TILE):
            rows = slice(h * blk, (h + 1) * blk)
            o = jnp.where(head_mask[h], o_all[rows], o)
            l = jnp.where(head_mask[h], lse[rows], l)
        for half in range(2):
            cols = slice(half * LANES, (half + 1) * LANES)
            osub[half, pl.ds(r0, blk), :] = o[:, cols]
            lsub[half, pl.ds(r0, blk), :] = l[:, cols]
        return carry

    lax.fori_loop(0, n_blocks, block, 0, unroll=2)

    for gi, (_, dil) in enumerate(ATTN_PAIRS):
        pl.when(g == gi)(functools.partial(scatter, dil))


def _attn(batch, seq_len, aq, ak, av):
    view = lambda a: a.reshape(batch, seq_len, ATTN_WIDTH)
    spec = pl.BlockSpec((None, seq_len, ATTN_GROUP_WIDTH), lambda bi, g: (bi, 0, g),
                        pipeline_mode=pl.Buffered(1))
    n_slabs = ATTN_WIDTH // LANES
    out_spec = pl.BlockSpec((2, None, seq_len, LANES), lambda bi, g: (g, bi, 0, 0))
    shape = jax.ShapeDtypeStruct((n_slabs, batch, seq_len, LANES), F32)
    sub_rows = seq_len + ATTN_BLOCK
    o, l = pl.pallas_call(
        functools.partial(_attn_body, seq_len),
        grid=(batch, ATTN_GROUPS),
        in_specs=[spec] * 3,
        out_specs=[out_spec] * 2,
        out_shape=[shape] * 2,
        scratch_shapes=[
            pltpu.VMEM((seq_len, ATTN_GROUP_WIDTH), BF16),
            pltpu.VMEM((sub_rows, ATTN_GROUP_WIDTH), BF16),
            pltpu.VMEM((sub_rows, ATTN_GROUP_WIDTH), BF16),
            pltpu.VMEM((2, seq_len, LANES), F32),
            pltpu.VMEM((2, seq_len, LANES), F32),
        ],
        compiler_params=pltpu.CompilerParams(
            dimension_semantics=("parallel", "arbitrary"), vmem_limit_bytes=VMEM_LIMIT_BYTES),
        name="attn",
    )(view(aq), view(ak), view(av))
    return (o.reshape(n_slabs, batch * seq_len, LANES), l.reshape(n_slabs, batch * seq_len, LANES))


def _merge_body(x_ref, wkv_ref, bonus_ref, g_ref, gate_ref, o_ref, l_ref,
                lnw_ref, lnb_ref, ones_ref, wpr_ref, wpa_ref, wo_ref, out_ref):
    ones_bd = ones_ref[...]
    inv_n = 1.0 / HEAD_DIM
    wkv = wkv_ref[...].astype(F32)
    dev = wkv - _head_sums(wkv, ones_bd) * inv_n
    var = _head_sums(dev * dev, ones_bd) * inv_n
    y = dev * lax.rsqrt(var + GN_EPS) * lnw_ref[...] + lnb_ref[...]
    y_a = ((y + bonus_ref[...].astype(F32)) * g_ref[...].astype(F32)).astype(BF16)
    pa = _dot(y_a, wpr_ref[...])
    gw = ATTN_GROUP_WIDTH
    slab = lambda ref, gi: jnp.concatenate([ref[2 * gi], ref[2 * gi + 1]], axis=-1)
    l0, l1, l2 = slab(l_ref, 0), slab(l_ref, 1), slab(l_ref, 2)
    mx = jnp.maximum(jnp.maximum(l0, l1), l2)
    e0, e1, e2 = jnp.exp(l0 - mx), jnp.exp(l1 - mx), jnp.exp(l2 - mx)
    inv = 1.0 / (e0 + e1 + e2)
    pb = None
    for gi, e in enumerate((e0, e1, e2)):
        yb = (slab(o_ref, gi) * (e * inv)).astype(BF16)
        part = _dot(yb, wpa_ref[gi * gw:(gi + 1) * gw, :])
        pb = part if pb is None else pb + part
    gates = gate_ref[...].astype(F32)
    merged = gates[:, :D_MODEL] * pa + gates[:, D_MODEL:] * pb
    out_ref[...] = x_ref[...] + _dot(merged.astype(BF16), wo_ref[...])


def _merge(x, wkv, bonus, g, gates, attn_o, attn_l, ln_w, ln_b, ones_bd, wpr, wpa, wo):
    t = x.shape[0]
    tm = MERGE_TILE
    row = lambda n: pl.BlockSpec((tm, n), lambda i: (i, 0))
    small = (ln_w, ln_b, ones_bd, wpr, wpa, wo)
    return pl.pallas_call(
        _merge_body,
        grid=(t // tm,),
        in_specs=[row(D_MODEL)] * 4 + [row(2 * D_MODEL)]
                 + [pl.BlockSpec((ATTN_WIDTH // LANES, tm, LANES), lambda i: (0, i, 0))] * 2
                 + [_resident(a.shape) for a in small],
        out_specs=row(D_MODEL),
        out_shape=jax.ShapeDtypeStruct((t, D_MODEL), F32),
        compiler_params=pltpu.CompilerParams(
            dimension_semantics=("parallel",), vmem_limit_bytes=VMEM_LIMIT_BYTES),
        name="merge",
    )(x, wkv, bonus, g, gates, attn_o, attn_l, *small)


def _pad_rows(a, n):
    return jnp.pad(a, ((0, n - a.shape[0]), (0, 0)))


def _layer(x, seq_len, ffn1_norm, ffn1_w_in, ffn1_w_out, mix_norm, w_in, b_gate, rwkv_mu,
           rwkv_w0, rwkv_w2, rwkv_a0, rwkv_a2, rwkv_g2, rwkv_k_k, rwkv_k_a, rwkv_r_k,
           rwkv_ln_w, rwkv_ln_b, attn_q_norm, attn_k_norm, w_proj_rwkv, w_proj_attn,
           w_out, ffn2_norm, ffn2_w_in, ffn2_w_out):
    batch = x.shape[0] // seq_len
    row = lambda a: a.reshape(1, -1)
    w3 = 3 * RWKV_WIDTH
    lora_cols = DECAY_LORA + AAA_LORA

    zeros = lambda n: jnp.zeros((D_MODEL, n), w_in.dtype)
    w_pad = jnp.concatenate([
        w_in[:, :w3 + lora_cols + GATE_LORA], zeros(GATE_LORA_PAD - GATE_LORA),
        w_in[:, RWKV_COLS:]], axis=1).astype(BF16)
    mu_pad = jnp.pad(rwkv_mu, (0, RW_PAD - RWKV_COLS)).reshape(1, RW_PAD)
    w2_pad = _pad_rows(rwkv_w2, LORA_PAD).astype(BF16)
    a2_pad = jnp.pad(rwkv_a2, ((DECAY_LORA, 0), (0, 0))).astype(BF16)
    g2_pad = _pad_rows(rwkv_g2, GATE_LORA_PAD).astype(BF16)
    n_heads = ATTN_WIDTH // HEAD_DIM
    q_gain = row(jnp.tile(attn_q_norm, n_heads) * HEAD_DIM ** -0.5)
    k_gain = row(jnp.tile(attn_k_norm, n_heads))
    idx = jnp.arange(MXU_DIM) // HEAD_DIM
    ones_bd = (idx[:, None] == idx[None, :]).astype(BF16)

    x = _ffn(x, row(ffn1_norm), ffn1_w_in.astype(BF16), ffn1_w_out.astype(BF16))
    (r, lw, k, v, kkn, b, bonus, g, aq, ak, av, gates) = _proj(
        x, seq_len, row(mix_norm), w_pad, row(b_gate), mu_pad, row(rwkv_w0), w2_pad,
        row(rwkv_a0), a2_pad, g2_pad, row(rwkv_k_k), row(rwkv_k_a), row(rwkv_r_k),
        q_gain, k_gain, ones_bd)
    wkv = _wkv(batch, seq_len, r, lw, k, v, kkn, b)
    attn_o, attn_l = _attn(batch, seq_len, aq, ak, av)
    x = _merge(x, wkv, bonus, g, gates, attn_o, attn_l, row(rwkv_ln_w), row(rwkv_ln_b), ones_bd,
               w_proj_rwkv.astype(BF16), w_proj_attn.astype(BF16), w_out.astype(BF16))
    return _ffn(x, row(ffn2_norm), ffn2_w_in.astype(BF16), ffn2_w_out.astype(BF16))


def kernel(x, ffn1_norm, ffn1_w_in, ffn1_w_out, mix_norm, w_in, b_gate, rwkv_mu, rwkv_w0, rwkv_w2, rwkv_a0, rwkv_a2, rwkv_g2, rwkv_k_k, rwkv_k_a, rwkv_r_k, rwkv_ln_w, rwkv_ln_b, attn_q_norm, attn_k_norm, w_proj_rwkv, w_proj_attn, w_out, ffn2_norm, ffn2_w_in, ffn2_w_out):
    batch, seq_len, d = x.shape
    params = (ffn1_norm, ffn1_w_in, ffn1_w_out, mix_norm, w_in, b_gate, rwkv_mu, rwkv_w0, rwkv_w2,
              rwkv_a0, rwkv_a2, rwkv_g2, rwkv_k_k, rwkv_k_a, rwkv_r_k, rwkv_ln_w, rwkv_ln_b,
              attn_q_norm, attn_k_norm, w_proj_rwkv, w_proj_attn, w_out, ffn2_norm, ffn2_w_in,
              ffn2_w_out)
    h = x.reshape(batch * seq_len, d)
    for layer in range(ffn1_norm.shape[0]):
        h = _layer(h, seq_len, *(p[layer] for p in params))
    return h.reshape(batch, seq_len, d)
```

```python
import functools

import jax
import jax.numpy as jnp
from jax import lax
from jax.experimental import pallas as pl
from jax.experimental.pallas import tpu as pltpu

F32 = jnp.float32
BF16 = jnp.bfloat16

D_MODEL = 1024
HEAD_DIM = 64
RWKV_WIDTH = 1024
DECAY_LORA = 64
AAA_LORA = 64
GATE_LORA = 160
RWKV_COLS = 3 * RWKV_WIDTH + DECAY_LORA + AAA_LORA + GATE_LORA
GN_EPS = 64e-5
ATTN_PAIRS = ((128, 1), (512, 4), (2048, 16))
ATTN_GROUPS = 3
ATTN_GROUP_WIDTH = 256
ATTN_WIDTH = ATTN_GROUPS * ATTN_GROUP_WIDTH
ATTN_BLOCK = 128
D_FF = 2816
RMS_EPS = 1e-6
NEG_INF = -1e30

LANES = 128
SUBLANES = 8
MXU_DIM = 256
HEADS_PER_TILE = MXU_DIM // HEAD_DIM
VMEM_LIMIT_BYTES = 56 * 1024 * 1024

LORA_PAD = 128
GATE_LORA_PAD = 256
RW_PAD = 3 * RWKV_WIDTH + LORA_PAD + GATE_LORA_PAD
ATT0 = RW_PAD
GATE0 = ATT0 + 3 * ATTN_WIDTH
IN_PAD = GATE0 + 2 * D_MODEL

FF_CHUNK = MXU_DIM
FFN_TILE = 512
PROJ_TILE = 256
MERGE_TILE = 512
WKV_CHUNK = 64
WKV_LOG2_CHUNK = 6
ATTN_UNROLL = 4
WKV_SEQS = 2


def _dot(a, b):
    return jnp.dot(a, b, preferred_element_type=F32)


def _dot_nt(a, b):
    return lax.dot_general(a, b, (((1,), (1,)), ((), ())), preferred_element_type=F32)


def _dot_tn(a, b):
    return lax.dot_general(a, b, (((0,), (0,)), ((), ())), preferred_element_type=F32)


def _rms_norm_bf16(x, gain):
    ms = jnp.mean(x * x, axis=-1, keepdims=True)
    return (x * lax.rsqrt(ms + RMS_EPS) * gain).astype(BF16)


def _head_sums(t, ones_bd):
    parts = []
    for q in range(t.shape[-1] // MXU_DIM):
        parts.append(_dot(t[:, q * MXU_DIM:(q + 1) * MXU_DIM].astype(BF16), ones_bd))
    return jnp.concatenate(parts, axis=-1)


def _resident(shape):
    return pl.BlockSpec(shape, lambda *_: (0,) * len(shape), pipeline_mode=pl.Buffered(1))


def _ffn_body(x_ref, gain_ref, w_in_ref, w_out_ref, o_ref):
    x = x_ref[...]
    h = _rms_norm_bf16(x, gain_ref[...])
    acc = None
    for c in range(D_FF // FF_CHUNK):
        lo = c * FF_CHUNK
        gate = _dot(h, w_in_ref[:, lo:lo + FF_CHUNK])
        up = _dot(h, w_in_ref[:, D_FF + lo:D_FF + lo + FF_CHUNK])
        act = (gate * jax.nn.sigmoid(gate) * up).astype(BF16)
        part = _dot(act, w_out_ref[lo:lo + FF_CHUNK, :])
        acc = part if acc is None else acc + part
    o_ref[...] = x + 0.5 * acc


def _ffn(x, gain, w_in, w_out):
    t = x.shape[0]
    tm = FFN_TILE
    return pl.pallas_call(
        _ffn_body,
        grid=(t // tm,),
        in_specs=[
            pl.BlockSpec((tm, D_MODEL), lambda i: (i, 0)),
            _resident((1, D_MODEL)),
            _resident((D_MODEL, 2 * D_FF)),
            _resident((D_FF, D_MODEL)),
        ],
        out_specs=pl.BlockSpec((tm, D_MODEL), lambda i: (i, 0)),
        out_shape=jax.ShapeDtypeStruct((t, D_MODEL), F32),
        compiler_params=pltpu.CompilerParams(
            dimension_semantics=("parallel",), vmem_limit_bytes=VMEM_LIMIT_BYTES),
        name="ffn",
    )(x, gain, w_in, w_out)


def _softplus(z):
    return jnp.maximum(z, 0.0) + jnp.log(1.0 + jnp.exp(-jnp.abs(z)))


def _proj_body(tiles_per_seq,
               x_ref, gain_ref, w_ref, bg_ref, mu_ref, w0_ref, w2_ref, a0_ref, a2_ref, g2_ref,
               kk_ref, ka_ref, rk_ref, qg_ref, kg_ref, ones_ref,
               r_o, lw_o, k_o, v_o, kkn_o, b_o, bonus_o, g_o, aq_o, ak_o, av_o, gate_o,
               carry_ref):
    i = pl.program_id(0)
    tm = x_ref.shape[0]
    w = RWKV_WIDTH
    h = _rms_norm_bf16(x_ref[...], gain_ref[...])
    ones_bd = ones_ref[...]
    row8 = lax.broadcasted_iota(jnp.int32, (SUBLANES, 1), 0)

    @pl.when((i % tiles_per_seq) == 0)
    def _():
        carry_ref[...] = jnp.zeros_like(carry_ref)

    def mm(lo, hi):
        return _dot(h, w_ref[:, lo:hi])

    def shift(p, lo, hi):
        rolled = pltpu.roll(p, 1, 0)
        head = jnp.where(row8 == 0, carry_ref[:, lo:hi], rolled[:SUBLANES])
        prev = jnp.concatenate([head, rolled[SUBLANES:]], axis=0)
        carry_ref[:, lo:hi] = p[tm - 1:tm, :]
        return p + (prev - p) * mu_ref[:, lo:hi]

    p_l = mm(3 * w, RW_PAD)
    p_k = mm(w, 2 * w)

    ps_l = shift(p_l, 3 * w, RW_PAD)
    lora = ps_l[:, :LORA_PAD]
    gd = ps_l[:, LORA_PAD:]
    wraw = w0_ref[...] + _dot(jnp.tanh(lora).astype(BF16), w2_ref[...])
    lw_o[...] = -jnp.exp(-_softplus(-wraw) - 0.5)
    a = jax.nn.sigmoid(a0_ref[...] + _dot(lora.astype(BF16), a2_ref[...]))
    g_o[...] = _dot(jax.nn.sigmoid(gd).astype(BF16), g2_ref[...]).astype(BF16)

    p_r = mm(0, w)

    k0 = shift(p_k, w, 2 * w)
    kk = k0 * kk_ref[...]
    kkn = kk * lax.rsqrt(jnp.maximum(_head_sums(kk * kk, ones_bd), 1e-24))
    k = k0 * (1.0 + (a - 1.0) * ka_ref[...])
    k_o[...] = k.astype(BF16)
    kkn_o[...] = kkn.astype(BF16)
    b_o[...] = (kkn * a).astype(BF16)

    p_v = mm(2 * w, 3 * w)

    r = shift(p_r, 0, w)
    r_o[...] = r.astype(BF16)
    rk_sum = _head_sums(r * k * rk_ref[...], ones_bd)

    pa_qk = mm(ATT0, ATT0 + 2 * ATTN_WIDTH)

    v = shift(p_v, 2 * w, 3 * w)
    v_o[...] = v.astype(BF16)
    bonus_o[...] = (rk_sum * v).astype(BF16)

    pa_v = mm(ATT0 + 2 * ATTN_WIDTH, GATE0)
    pg_a = mm(GATE0, GATE0 + D_MODEL)

    pq = pa_qk[:, 0:ATTN_WIDTH]
    pk = pa_qk[:, ATTN_WIDTH:]
    inv_n = 1.0 / HEAD_DIM
    qn = pq * lax.rsqrt(_head_sums(pq * pq, ones_bd) * inv_n + RMS_EPS) * qg_ref[...]
    kn = pk * lax.rsqrt(_head_sums(pk * pk, ones_bd) * inv_n + RMS_EPS) * kg_ref[...]
    aq_o[...] = qn.astype(BF16)
    ak_o[...] = kn.astype(BF16)
    av_o[...] = pa_v.astype(BF16)

    pg_b = mm(GATE0 + D_MODEL, IN_PAD)
    gate_o[:, :D_MODEL] = jax.nn.sigmoid(pg_a + bg_ref[:, :D_MODEL]).astype(BF16)
    gate_o[:, D_MODEL:] = jax.nn.sigmoid(pg_b + bg_ref[:, D_MODEL:]).astype(BF16)


def _proj(x, seq_len, gain, w, bg, mu, w0, w2, a0, a2, g2, k_k, k_a, r_k, qg, kg, ones_bd):
    t = x.shape[0]
    tm = PROJ_TILE
    row = lambda n: pl.BlockSpec((tm, n), lambda i: (i, 0))
    wide = jax.ShapeDtypeStruct((t, RWKV_WIDTH), BF16)
    wide32 = jax.ShapeDtypeStruct((t, RWKV_WIDTH), F32)
    att = jax.ShapeDtypeStruct((t, ATTN_WIDTH), BF16)
    return pl.pallas_call(
        functools.partial(_proj_body, seq_len // tm),
        grid=(t // tm,),
        in_specs=[row(D_MODEL)] + [_resident(a.shape) for a in
                                   (gain, w, bg, mu, w0, w2, a0, a2, g2, k_k, k_a, r_k, qg, kg, ones_bd)],
        out_specs=[row(RWKV_WIDTH)] * 8 + [row(ATTN_WIDTH)] * 3 + [row(2 * D_MODEL)],
        out_shape=[wide, wide32] + [wide] * 6 + [att] * 3
                  + [jax.ShapeDtypeStruct((t, 2 * D_MODEL), BF16)],
        scratch_shapes=[pltpu.VMEM((1, RW_PAD), F32)],
        compiler_params=pltpu.CompilerParams(
            dimension_semantics=("arbitrary",), vmem_limit_bytes=VMEM_LIMIT_BYTES),
        name="proj",
    )(x, gain, w, bg, mu, w0, w2, a0, a2, g2, k_k, k_a, r_k, qg, kg, ones_bd)


def _wkv_body(r_ref, lw_ref, k_ref, v_ref, kkn_ref, b_ref, y_ref, z_ref):
    c_len = WKV_CHUNK
    n_seq = r_ref.shape[0]
    n_tiles = RWKV_WIDTH // MXU_DIM

    @pl.when(pl.program_id(1) == 0)
    def _():
        z_ref[...] = jnp.zeros_like(z_ref)

    ri = lax.broadcasted_iota(jnp.int32, (c_len, c_len), 0)
    ci = lax.broadcasted_iota(jnp.int32, (c_len, c_len), 1)
    tri = jnp.where(ci <= ri, 1.0, 0.0).astype(BF16)
    lane = lax.broadcasted_iota(jnp.int32, (c_len, MXU_DIM), 1)
    rowc = lax.broadcasted_iota(jnp.int32, (c_len, MXU_DIM), 0)
    head_mask = [(lane // HEAD_DIM) == h for h in range(HEADS_PER_TILE)]
    blk_mask = [(lane // c_len) == h for h in range(HEADS_PER_TILE)]
    col_in_blk = lane % c_len
    strict = col_in_blk < rowc
    incl = col_in_blk <= rowc
    eye = jnp.where(col_in_blk == rowc, 1.0, 0.0)
    br = lax.broadcasted_iota(jnp.int32, (MXU_DIM, MXU_DIM), 0) // HEAD_DIM
    bc = lax.broadcasted_iota(jnp.int32, (MXU_DIM, MXU_DIM), 1) // HEAD_DIM
    bd_mask = br == bc

    def stack(x, masks):
        return jnp.concatenate([jnp.where(m, x, 0.0) for m in masks], axis=0).astype(BF16)

    def rows(*xs):
        return jnp.concatenate(xs, axis=0)

    units = []
    for s in range(n_seq):
        lw = lw_ref[s]
        hi = lw.astype(BF16)
        rem = lw - hi.astype(F32)
        mid = rem.astype(BF16)
        lo = (rem - mid.astype(F32)).astype(BF16)
        cum = _dot(tri, hi) + _dot(tri, mid) + _dot(tri, lo)
        ref = cum[c_len // 2 - 1:c_len // 2, :]
        e_fwd = jnp.exp(cum - ref)
        e_prev = jnp.exp(cum - lw - ref)
        e_bwd = jnp.exp(ref - cum)
        e_ref = jnp.exp(ref)
        lam_end = e_fwd[c_len - 1:c_len, :]
        lam_c = jnp.exp(cum[c_len - 1:c_len, :])
        at = -kkn_ref[s].astype(F32) * e_prev
        rt = r_ref[s].astype(F32) * e_fwd
        kt = k_ref[s].astype(F32) * e_bwd
        bt = b_ref[s].astype(F32) * e_bwd
        v = v_ref[s].astype(F32)
        for q in range(n_tiles):
            sl = slice(q * MXU_DIM, (q + 1) * MXU_DIM)
            units.append(dict(
                s=s, sl=sl, zi=s * n_tiles + q, at=at[:, sl], rt=rt[:, sl], kt=kt[:, sl],
                bt=bt[:, sl], v=v[:, sl], e_ref=e_ref[:, sl], lam_end=lam_end[:, sl],
                lam_c=lam_c[:, sl]))

    for u in units:
        ar = rows(u["at"], u["rt"]).astype(BF16)
        u["sb"] = _dot_nt(ar, stack(u["bt"], head_mask))
        u["sk"] = _dot_nt(ar, stack(u["kt"], head_mask))
    for u in units:
        u["a_ab"] = jnp.where(strict, u["sb"][:c_len], 0.0)
        u["a_rb"] = jnp.where(incl, u["sb"][c_len:], 0.0).astype(BF16)
        a_ak = jnp.where(strict, u["sk"][:c_len], 0.0)
        a_rk = jnp.where(incl, u["sk"][c_len:], 0.0)
        u["xv"] = _dot(rows(a_ak, a_rk).astype(BF16), stack(u["v"], head_mask))
        u["pw"] = _dot(u["a_ab"].astype(BF16), stack(u["a_ab"], blk_mask))
        u["t_inv"] = eye + u["a_ab"]
    for lev in range(1, WKV_LOG2_CHUNK):
        for u in units:
            rhs = stack(u["pw"], blk_mask)
            if lev < WKV_LOG2_CHUNK - 1:
                prod = _dot(rows(u["pw"], u["t_inv"]).astype(BF16), rhs)
                u["pw"] = prod[:c_len]
                u["t_inv"] = u["t_inv"] + prod[c_len:]
            else:
                u["t_inv"] = u["t_inv"] + _dot(u["t_inv"].astype(BF16), rhs)
    for u in units:
        t16 = u["t_inv"].astype(BF16)
        u["wt"] = _dot(t16, stack(u["at"] * u["e_ref"], head_mask))
        u["u0"] = _dot(t16, stack(u["xv"][:c_len], head_mask))
    for u in units:
        u["qh"] = u["rt"] * u["e_ref"] + _dot(u["a_rb"], stack(u["wt"], head_mask))
        u["y0"] = _dot(u["a_rb"], stack(u["u0"], head_mask)) + u["xv"][c_len:]
        bh = (u["bt"] * u["lam_end"]).astype(BF16)
        kh = (u["kt"] * u["lam_end"]).astype(BF16)
        u["m_lr"] = _dot_tn(u["wt"].astype(BF16), bh)
        u["g"] = _dot_tn(rows(u["u0"], u["v"]).astype(BF16), rows(bh, kh))
    for u in units:
        z = z_ref[u["zi"]]
        z16 = z.astype(BF16)
        y_ref[u["s"], :, u["sl"]] = (_dot_nt(u["qh"].astype(BF16), z16) + u["y0"]).astype(BF16)
        z_ref[u["zi"]] = (z * u["lam_c"]
                          + _dot(z16, jnp.where(bd_mask, u["m_lr"], 0.0).astype(BF16))
                          + jnp.where(bd_mask, u["g"], 0.0))


def _wkv(batch, seq_len, r, lw, k, v, kkn, b):
    n_chunks = seq_len // WKV_CHUNK
    n_seq = WKV_SEQS if batch % WKV_SEQS == 0 else 1
    view = lambda a: a.reshape(batch, seq_len, RWKV_WIDTH)
    blk = pl.BlockSpec((n_seq, WKV_CHUNK, RWKV_WIDTH), lambda bi, ci: (bi, ci, 0))
    y = pl.pallas_call(
        _wkv_body,
        grid=(batch // n_seq, n_chunks),
        in_specs=[blk] * 6,
        out_specs=blk,
        out_shape=jax.ShapeDtypeStruct((batch, seq_len, RWKV_WIDTH), BF16),
        scratch_shapes=[pltpu.VMEM((n_seq * RWKV_WIDTH // MXU_DIM, MXU_DIM, MXU_DIM), F32)],
        compiler_params=pltpu.CompilerParams(
            dimension_semantics=("arbitrary", "arbitrary"), vmem_limit_bytes=VMEM_LIMIT_BYTES),
        name="wkv",
    )(*(view(a) for a in (r, lw, k, v, kkn, b)))
    return y.reshape(batch * seq_len, RWKV_WIDTH)


def _attn_body(seq_len, q_ref, k_ref, v_ref, o_ref, l_ref, qsub, ksub, vsub, osub, lsub):
    g = pl.program_id(1)
    blk = ATTN_BLOCK
    n_blocks = seq_len // blk

    def regroup(dil):
        sub = seq_len // dil
        for src, dst, off in ((q_ref, qsub, 0), (k_ref, ksub, blk), (v_ref, vsub, blk)):
            if dil == 1:
                dst[off:off + seq_len, :] = src[...]
            else:
                for half in range(2):
                    lsub[half] = src[:, half * LANES:(half + 1) * LANES].astype(F32)
                for c in range(dil):
                    for r0 in range(0, sub, blk):
                        for half in range(2):
                            rows = slice(off + c * sub + r0, off + c * sub + r0 + blk)
                            dst[rows, half * LANES:(half + 1) * LANES] = (
                                lsub[half, pl.ds(c + r0 * dil, blk, stride=dil), :].astype(BF16))

    def scatter(dil):
        sub = seq_len // dil
        for src, dst in ((osub, o_ref), (lsub, l_ref)):
            for half in range(2):
                if dil == 1:
                    dst[half] = src[half]
                else:
                    for c in range(dil):
                        for r0 in range(0, sub, blk):
                            dst[half, pl.ds(c + r0 * dil, blk, stride=dil), :] = (
                                src[half, c * sub + r0:c * sub + r0 + blk, :])

    ksub[0:blk, :] = jnp.zeros((blk, ATTN_GROUP_WIDTH), BF16)
    vsub[0:blk, :] = jnp.zeros((blk, ATTN_GROUP_WIDTH), BF16)
    for gi, (_, dil) in enumerate(ATTN_PAIRS):
        pl.when(g == gi)(functools.partial(regroup, dil))

    blocks_per_sub = jnp.int32(n_blocks)
    for gi, (_, dil) in enumerate(ATTN_PAIRS):
        blocks_per_sub = jnp.where(g == gi, n_blocks // dil, blocks_per_sub)

    qi = lax.broadcasted_iota(jnp.int32, (blk, 2 * blk), 0)
    kj = lax.broadcasted_iota(jnp.int32, (blk, 2 * blk), 1)
    dist = qi + blk - kj
    bias_any = jnp.where((dist >= 0) & (dist <= blk), 0.0, NEG_INF)
    bias_first = jnp.where(kj >= blk, bias_any, NEG_INF)
    lane = lax.broadcasted_iota(jnp.int32, (blk, ATTN_GROUP_WIDTH), 1)
    head_mask = [(lane // HEAD_DIM) == h for h in range(HEADS_PER_TILE)]

    def blocks(it, carry):
        ids = [it * ATTN_UNROLL + j for j in range(ATTN_UNROLL)]
        r0s = [pl.multiple_of(i * blk, blk) for i in ids]
        ss, ps, dens, ms, os_ = [], [], [], [], []
        for i, r0 in zip(ids, r0s):
            q = qsub[pl.ds(r0, blk), :]
            qs = jnp.concatenate([jnp.where(m, q, jnp.zeros_like(q)) for m in head_mask], axis=0)
            bias = jnp.where((i % blocks_per_sub) > 0, bias_any, bias_first)
            s_ = _dot_nt(qs, ksub[pl.ds(r0, 2 * blk), :])
            ss.append((s_.reshape(HEADS_PER_TILE, blk, 2 * blk) + bias[None]).reshape(s_.shape))
        for s_ in ss:
            ms.append(jnp.max(s_, axis=-1, keepdims=True))
        for s_, m in zip(ss, ms):
            ps.append(jnp.exp(s_ - m))
        for p in ps:
            dens.append(jnp.sum(p, axis=-1, keepdims=True))
        for p, r0 in zip(ps, r0s):
            os_.append(_dot(p.astype(BF16), vsub[pl.ds(r0, 2 * blk), :]))
        for o_all, den, m, r0 in zip(os_, dens, ms, r0s):
            o_all = o_all * (1.0 / den)
            lse = m + jnp.log(den)
            o = jnp.zeros((blk, ATTN_GROUP_WIDTH), F32)
            l = jnp.zeros((blk, ATTN_GROUP_WIDTH), F32)
            for h in range(HEADS_PER_TILE):
                rows = slice(h * blk, (h + 1) * blk)
                o = jnp.where(head_mask[h], o_all[rows], o)
                l = jnp.where(head_mask[h], lse[rows], l)
            for half in range(2):
                cols = slice(half * LANES, (half + 1) * LANES)
                osub[half, pl.ds(r0, blk), :] = o[:, cols]
                lsub[half, pl.ds(r0, blk), :] = l[:, cols]
        return carry

    lax.fori_loop(0, n_blocks // ATTN_UNROLL, blocks, 0)

    for gi, (_, dil) in enumerate(ATTN_PAIRS):
        pl.when(g == gi)(functools.partial(scatter, dil))


def _attn(batch, seq_len, aq, ak, av):
    view = lambda a: a.reshape(batch, seq_len, ATTN_WIDTH)
    spec = pl.BlockSpec((None, seq_len, ATTN_GROUP_WIDTH), lambda bi, g: (bi, 0, g),
                        pipeline_mode=pl.Buffered(1))
    n_slabs = ATTN_WIDTH // LANES
    out_spec = pl.BlockSpec((2, None, seq_len, LANES), lambda bi, g: (g, bi, 0, 0))
    shape = jax.ShapeDtypeStruct((n_slabs, batch, seq_len, LANES), F32)
    sub_rows = seq_len + ATTN_BLOCK
    o, l = pl.pallas_call(
        functools.partial(_attn_body, seq_len),
        grid=(batch, ATTN_GROUPS),
        in_specs=[spec] * 3,
        out_specs=[out_spec] * 2,
        out_shape=[shape] * 2,
        scratch_shapes=[
            pltpu.VMEM((seq_len, ATTN_GROUP_WIDTH), BF16),
            pltpu.VMEM((sub_rows, ATTN_GROUP_WIDTH), BF16),
            pltpu.VMEM((sub_rows, ATTN_GROUP_WIDTH), BF16),
            pltpu.VMEM((2, seq_len, LANES), F32),
            pltpu.VMEM((2, seq_len, LANES), F32),
        ],
        compiler_params=pltpu.CompilerParams(
            dimension_semantics=("parallel", "arbitrary"), vmem_limit_bytes=VMEM_LIMIT_BYTES),
        name="attn",
    )(view(aq), view(ak), view(av))
    return (o.reshape(n_slabs, batch * seq_len, LANES), l.reshape(n_slabs, batch * seq_len, LANES))


def _merge_body(x_ref, wkv_ref, bonus_ref, g_ref, gate_ref, o_ref, l_ref,
                lnw_ref, lnb_ref, ones_ref, wpr_ref, wpa_ref, wo_ref, out_ref):
    ones_bd = ones_ref[...]
    inv_n = 1.0 / HEAD_DIM
    wkv = wkv_ref[...].astype(F32)
    dev = wkv - _head_sums(wkv, ones_bd) * inv_n
    var = _head_sums(dev * dev, ones_bd) * inv_n
    y = dev * lax.rsqrt(var + GN_EPS) * lnw_ref[...] + lnb_ref[...]
    y_a = ((y + bonus_ref[...].astype(F32)) * g_ref[...].astype(F32)).astype(BF16)
    pa = _dot(y_a, wpr_ref[...])
    gw = ATTN_GROUP_WIDTH
    slab = lambda ref, gi: jnp.concatenate([ref[2 * gi], ref[2 * gi + 1]], axis=-1)
    l0, l1, l2 = slab(l_ref, 0), slab(l_ref, 1), slab(l_ref, 2)
    mx = jnp.maximum(jnp.maximum(l0, l1), l2)
    e0, e1, e2 = jnp.exp(l0 - mx), jnp.exp(l1 - mx), jnp.exp(l2 - mx)
    inv = 1.0 / (e0 + e1 + e2)
    pb = None
    for gi, e in enumerate((e0, e1, e2)):
        yb = (slab(o_ref, gi) * (e * inv)).astype(BF16)
        part = _dot(yb, wpa_ref[gi * gw:(gi + 1) * gw, :])
        pb = part if pb is None else pb + part
    gates = gate_ref[...].astype(F32)
    merged = gates[:, :D_MODEL] * pa + gates[:, D_MODEL:] * pb
    out_ref[...] = x_ref[...] + _dot(merged.astype(BF16), wo_ref[...])


def _merge(x, wkv, bonus, g, gates, attn_o, attn_l, ln_w, ln_b, ones_bd, wpr, wpa, wo):
    t = x.shape[0]
    tm = MERGE_TILE
    row = lambda n: pl.BlockSpec((tm, n), lambda i: (i, 0))
    small = (ln_w, ln_b, ones_bd, wpr, wpa, wo)
    return pl.pallas_call(
        _merge_body,
        grid=(t // tm,),
        in_specs=[row(D_MODEL)] * 4 + [row(2 * D_MODEL)]
                 + [pl.BlockSpec((ATTN_WIDTH // LANES, tm, LANES), lambda i: (0, i, 0))] * 2
                 + [_resident(a.shape) for a in small],
        out_specs=row(D_MODEL),
        out_shape=jax.ShapeDtypeStruct((t, D_MODEL), F32),
        compiler_params=pltpu.CompilerParams(
            dimension_semantics=("parallel",), vmem_limit_bytes=VMEM_LIMIT_BYTES),
        name="merge",
    )(x, wkv, bonus, g, gates, attn_o, attn_l, *small)


def _pad_rows(a, n):
    return jnp.pad(a, ((0, n - a.shape[0]), (0, 0)))


def _layer(x, seq_len, ffn1_norm, ffn1_w_in, ffn1_w_out, mix_norm, w_in, b_gate, rwkv_mu,
           rwkv_w0, rwkv_w2, rwkv_a0, rwkv_a2, rwkv_g2, rwkv_k_k, rwkv_k_a, rwkv_r_k,
           rwkv_ln_w, rwkv_ln_b, attn_q_norm, attn_k_norm, w_proj_rwkv, w_proj_attn,
           w_out, ffn2_norm, ffn2_w_in, ffn2_w_out):
    batch = x.shape[0] // seq_len
    row = lambda a: a.reshape(1, -1)
    w3 = 3 * RWKV_WIDTH
    lora_cols = DECAY_LORA + AAA_LORA

    zeros = lambda n: jnp.zeros((D_MODEL, n), w_in.dtype)
    w_pad = jnp.concatenate([
        w_in[:, :w3 + lora_cols + GATE_LORA], zeros(GATE_LORA_PAD - GATE_LORA),
        w_in[:, RWKV_COLS:]], axis=1).astype(BF16)
    mu_pad = jnp.pad(rwkv_mu, (0, RW_PAD - RWKV_COLS)).reshape(1, RW_PAD)
    w2_pad = _pad_rows(rwkv_w2, LORA_PAD).astype(BF16)
    a2_pad = jnp.pad(rwkv_a2, ((DECAY_LORA, 0), (0, 0))).astype(BF16)
    g2_pad = _pad_rows(rwkv_g2, GATE_LORA_PAD).astype(BF16)
    n_heads = ATTN_WIDTH // HEAD_DIM
    q_gain = row(jnp.tile(attn_q_norm, n_heads) * HEAD_DIM ** -0.5)
    k_gain = row(jnp.tile(attn_k_norm, n_heads))
    idx = jnp.arange(MXU_DIM) // HEAD_DIM
    ones_bd = (idx[:, None] == idx[None, :]).astype(BF16)

    x = _ffn(x, row(ffn1_norm), ffn1_w_in.astype(BF16), ffn1_w_out.astype(BF16))
    (r, lw, k, v, kkn, b, bonus, g, aq, ak, av, gates) = _proj(
        x, seq_len, row(mix_norm), w_pad, row(b_gate), mu_pad, row(rwkv_w0), w2_pad,
        row(rwkv_a0), a2_pad, g2_pad, row(rwkv_k_k), row(rwkv_k_a), row(rwkv_r_k),
        q_gain, k_gain, ones_bd)
    wkv = _wkv(batch, seq_len, r, lw, k, v, kkn, b)
    attn_o, attn_l = _attn(batch, seq_len, aq, ak, av)
    x = _merge(x, wkv, bonus, g, gates, attn_o, attn_l, row(rwkv_ln_w), row(rwkv_ln_b), ones_bd,
               w_proj_rwkv.astype(BF16), w_proj_attn.astype(BF16), w_out.astype(BF16))
    return _ffn(x, row(ffn2_norm), ffn2_w_in.astype(BF16), ffn2_w_out.astype(BF16))


def kernel(x, ffn1_norm, ffn1_w_in, ffn1_w_out, mix_norm, w_in, b_gate, rwkv_mu, rwkv_w0, rwkv_w2, rwkv_a0, rwkv_a2, rwkv_g2, rwkv_k_k, rwkv_k_a, rwkv_r_k, rwkv_ln_w, rwkv_ln_b, attn_q_norm, attn_k_norm, w_proj_rwkv, w_proj_attn, w_out, ffn2_norm, ffn2_w_in, ffn2_w_out):
    batch, seq_len, d = x.shape
    params = (ffn1_norm, ffn1_w_in, ffn1_w_out, mix_norm, w_in, b_gate, rwkv_mu, rwkv_w0, rwkv_w2,
              rwkv_a0, rwkv_a2, rwkv_g2, rwkv_k_k, rwkv_k_a, rwkv_r_k, rwkv_ln_w, rwkv_ln_b,
              attn_q_norm, attn_k_norm, w_proj_rwkv, w_proj_attn, w_out, ffn2_norm, ffn2_w_in,
              ffn2_w_out)
    h = x.reshape(batch * seq_len, d)
    for layer in range(ffn1_norm.shape[0]):
        h = _layer(h, seq_len, *(p[layer] for p in params))
    return h.reshape(batch, seq_len, d)
```

```python
import functools

import jax
import jax.numpy as jnp
from jax import lax
from jax.experimental import pallas as pl
from jax.experimental.pallas import tpu as pltpu

F32 = jnp.float32
BF16 = jnp.bfloat16

D_MODEL = 1024
HEAD_DIM = 64
RWKV_WIDTH = 1024
DECAY_LORA = 64
AAA_LORA = 64
GATE_LORA = 160
RWKV_COLS = 3 * RWKV_WIDTH + DECAY_LORA + AAA_LORA + GATE_LORA
GN_EPS = 64e-5
ATTN_PAIRS = ((128, 1), (512, 4), (2048, 16))
ATTN_GROUPS = 3
ATTN_GROUP_WIDTH = 256
ATTN_WIDTH = ATTN_GROUPS * ATTN_GROUP_WIDTH
ATTN_BLOCK = 128
D_FF = 2816
RMS_EPS = 1e-6
NEG_INF = -1e30

LANES = 128
SUBLANES = 8
MXU_DIM = 256
HEADS_PER_TILE = MXU_DIM // HEAD_DIM
VMEM_LIMIT_BYTES = 56 * 1024 * 1024
MIX_VMEM_LIMIT_BYTES = 60 * 1024 * 1024

LORA_PAD = 128
GATE_LORA_PAD = 256
RW_PAD = 3 * RWKV_WIDTH + LORA_PAD + GATE_LORA_PAD
ATT0 = RW_PAD
GATE0 = ATT0 + 3 * ATTN_WIDTH
IN_PAD = GATE0 + 2 * D_MODEL

FF_CHUNK = MXU_DIM
FFN_TILE = 512
PROJ_TILE = 256
MERGE_TILE = 512
WKV_CHUNK = 64
WKV_LOG2_CHUNK = 6
ATTN_UNROLL = 4
MIX_SEQS = 2
MIX_STEPS_PER_PAIR = 8


def _dot(a, b):
    return jnp.dot(a, b, preferred_element_type=F32)


def _dot_nt(a, b):
    return lax.dot_general(a, b, (((1,), (1,)), ((), ())), preferred_element_type=F32)


def _dot_tn(a, b):
    return lax.dot_general(a, b, (((0,), (0,)), ((), ())), preferred_element_type=F32)


def _rms_norm_bf16(x, gain):
    ms = jnp.mean(x * x, axis=-1, keepdims=True)
    return (x * lax.rsqrt(ms + RMS_EPS) * gain).astype(BF16)


def _head_sums(t, ones_bd):
    parts = []
    for q in range(t.shape[-1] // MXU_DIM):
        parts.append(_dot(t[:, q * MXU_DIM:(q + 1) * MXU_DIM].astype(BF16), ones_bd))
    return jnp.concatenate(parts, axis=-1)


def _resident(shape):
    return pl.BlockSpec(shape, lambda *_: (0,) * len(shape), pipeline_mode=pl.Buffered(1))


def _ffn_body(x_ref, gain_ref, w_in_ref, w_out_ref, o_ref):
    x = x_ref[...]
    h = _rms_norm_bf16(x, gain_ref[...])
    acc = None
    for c in range(D_FF // FF_CHUNK):
        lo = c * FF_CHUNK
        gate = _dot(h, w_in_ref[:, lo:lo + FF_CHUNK])
        up = _dot(h, w_in_ref[:, D_FF + lo:D_FF + lo + FF_CHUNK])
        act = (gate * jax.nn.sigmoid(gate) * up).astype(BF16)
        part = _dot(act, w_out_ref[lo:lo + FF_CHUNK, :])
        acc = part if acc is None else acc + part
    o_ref[...] = x + 0.5 * acc


def _ffn(x, gain, w_in, w_out):
    t = x.shape[0]
    tm = FFN_TILE
    return pl.pallas_call(
        _ffn_body,
        grid=(t // tm,),
        in_specs=[
            pl.BlockSpec((tm, D_MODEL), lambda i: (i, 0)),
            _resident((1, D_MODEL)),
            _resident((D_MODEL, 2 * D_FF)),
            _resident((D_FF, D_MODEL)),
        ],
        out_specs=pl.BlockSpec((tm, D_MODEL), lambda i: (i, 0)),
        out_shape=jax.ShapeDtypeStruct((t, D_MODEL), F32),
        compiler_params=pltpu.CompilerParams(
            dimension_semantics=("parallel",), vmem_limit_bytes=VMEM_LIMIT_BYTES),
        name="ffn",
    )(x, gain, w_in, w_out)


def _softplus(z):
    return jnp.maximum(z, 0.0) + jnp.log(1.0 + jnp.exp(-jnp.abs(z)))


def _proj_body(tiles_per_seq,
               x_ref, gain_ref, w_ref, bg_ref, mu_ref, w0_ref, w2_ref, a0_ref, a2_ref, g2_ref,
               kk_ref, ka_ref, rk_ref, qg_ref, kg_ref, ones_ref,
               r_o, lw_o, k_o, v_o, kkn_o, b_o, bonus_o, g_o, aq_o, ak_o, av_o, gate_o,
               carry_ref):
    i = pl.program_id(0)
    tm = x_ref.shape[0]
    w = RWKV_WIDTH
    sw = MXU_DIM
    h = _rms_norm_bf16(x_ref[...], gain_ref[...])
    ones_bd = ones_ref[...]
    row8 = lax.broadcasted_iota(jnp.int32, (SUBLANES, 1), 0)
    inv_n = 1.0 / HEAD_DIM

    @pl.when((i % tiles_per_seq) == 0)
    def _():
        carry_ref[...] = jnp.zeros_like(carry_ref)

    def mm(lo, hi):
        return _dot(h, w_ref[:, lo:hi])

    def shift(p, lo, hi):
        rolled = pltpu.roll(p, 1, 0)
        head = jnp.where(row8 == 0, carry_ref[:, lo:hi], rolled[:SUBLANES])
        prev = jnp.concatenate([head, rolled[SUBLANES:]], axis=0)
        carry_ref[:, lo:hi] = p[tm - 1:tm, :]
        return p + (prev - p) * mu_ref[:, lo:hi]

    def head_sum(t):
        return _dot(t.astype(BF16), ones_bd)

    def rwkv_mm(q):
        c = q * sw
        return [mm(j * w + c, j * w + c + sw) for j in range(3)]

    def rwkv_strip(q, prk, lora16, tanh16, sg16):
        c = q * sw
        cols = slice(c, c + sw)
        r = shift(prk[0], c, c + sw)
        k0 = shift(prk[1], w + c, w + c + sw)
        v = shift(prk[2], 2 * w + c, 2 * w + c + sw)
        wraw = w0_ref[:, cols] + _dot(tanh16, w2_ref[:, cols])
        lw_o[:, cols] = -jnp.exp(-_softplus(-wraw) - 0.5)
        a = jax.nn.sigmoid(a0_ref[:, cols] + _dot(lora16, a2_ref[:, cols]))
        g_o[:, cols] = _dot(sg16, g2_ref[:, cols]).astype(BF16)
        kk = k0 * kk_ref[:, cols]
        kkn = kk * lax.rsqrt(jnp.maximum(head_sum(kk * kk), 1e-24))
        k = k0 * (1.0 + (a - 1.0) * ka_ref[:, cols])
        r_o[:, cols] = r.astype(BF16)
        k_o[:, cols] = k.astype(BF16)
        v_o[:, cols] = v.astype(BF16)
        kkn_o[:, cols] = kkn.astype(BF16)
        b_o[:, cols] = (kkn * a).astype(BF16)
        bonus_o[:, cols] = (head_sum(r * k * rk_ref[:, cols]) * v).astype(BF16)

    def attn_mm(g):
        c = ATT0 + g * sw
        return [mm(c + j * ATTN_WIDTH, c + j * ATTN_WIDTH + sw) for j in range(3)]

    def attn_strip(g, pqkv):
        cols = slice(g * sw, (g + 1) * sw)
        pq, pk, pv = pqkv
        qn = pq * lax.rsqrt(head_sum(pq * pq) * inv_n + RMS_EPS) * qg_ref[:, cols]
        kn = pk * lax.rsqrt(head_sum(pk * pk) * inv_n + RMS_EPS) * kg_ref[:, cols]
        aq_o[:, cols] = qn.astype(BF16)
        ak_o[:, cols] = kn.astype(BF16)
        av_o[:, cols] = pv.astype(BF16)

    def gate_mm(q):
        return mm(GATE0 + q * 2 * sw, GATE0 + (q + 1) * 2 * sw)

    def gate_strip(q, pg):
        cols = slice(q * 2 * sw, (q + 1) * 2 * sw)
        gate_o[:, cols] = jax.nn.sigmoid(pg + bg_ref[:, cols]).astype(BF16)

    p_l = mm(3 * w, RW_PAD)
    nxt = rwkv_mm(0)
    ps_l = shift(p_l, 3 * w, RW_PAD)
    lora = ps_l[:, :LORA_PAD]
    lora16 = lora.astype(BF16)
    tanh16 = jnp.tanh(lora).astype(BF16)
    sg16 = jax.nn.sigmoid(ps_l[:, LORA_PAD:]).astype(BF16)

    n_rw = w // sw
    n_gate = 2 * D_MODEL // (2 * sw)
    for q in range(n_rw):
        cur = nxt
        nxt = rwkv_mm(q + 1) if q + 1 < n_rw else attn_mm(0)
        rwkv_strip(q, cur, lora16, tanh16, sg16)
    for g in range(ATTN_GROUPS):
        cur = nxt
        nxt = attn_mm(g + 1) if g + 1 < ATTN_GROUPS else gate_mm(0)
        attn_strip(g, cur)
    for q in range(n_gate):
        cur = nxt
        nxt = gate_mm(q + 1) if q + 1 < n_gate else None
        gate_strip(q, cur)


def _proj(x, seq_len, gain, w, bg, mu, w0, w2, a0, a2, g2, k_k, k_a, r_k, qg, kg, ones_bd):
    t = x.shape[0]
    tm = PROJ_TILE
    row = lambda n: pl.BlockSpec((tm, n), lambda i: (i, 0))
    wide = jax.ShapeDtypeStruct((t, RWKV_WIDTH), BF16)
    wide32 = jax.ShapeDtypeStruct((t, RWKV_WIDTH), F32)
    att = jax.ShapeDtypeStruct((t, ATTN_WIDTH), BF16)
    return pl.pallas_call(
        functools.partial(_proj_body, seq_len // tm),
        grid=(t // tm,),
        in_specs=[row(D_MODEL)] + [_resident(a.shape) for a in
                                   (gain, w, bg, mu, w0, w2, a0, a2, g2, k_k, k_a, r_k, qg, kg, ones_bd)],
        out_specs=[row(RWKV_WIDTH)] * 8 + [row(ATTN_WIDTH)] * 3 + [row(2 * D_MODEL)],
        out_shape=[wide, wide32] + [wide] * 6 + [att] * 3
                  + [jax.ShapeDtypeStruct((t, 2 * D_MODEL), BF16)],
        scratch_shapes=[pltpu.VMEM((1, RW_PAD), F32)],
        compiler_params=pltpu.CompilerParams(
            dimension_semantics=("arbitrary",), vmem_limit_bytes=VMEM_LIMIT_BYTES),
        name="proj",
    )(x, gain, w, bg, mu, w0, w2, a0, a2, g2, k_k, k_a, r_k, qg, kg, ones_bd)


def _wkv_stages(r_ref, lw_ref, k_ref, v_ref, kkn_ref, b_ref, y_ref, z_ref):
    c_len = WKV_CHUNK
    n_seq = r_ref.shape[0]
    n_tiles = RWKV_WIDTH // MXU_DIM

    @pl.when(pl.program_id(1) == 0)
    def _():
        z_ref[...] = jnp.zeros_like(z_ref)

    trow = lax.broadcasted_iota(jnp.int32, (c_len, 1), 0)
    lane = lax.broadcasted_iota(jnp.int32, (c_len, MXU_DIM), 1)
    rowc = lax.broadcasted_iota(jnp.int32, (c_len, MXU_DIM), 0)
    head_mask = [(lane // HEAD_DIM) == h for h in range(HEADS_PER_TILE)]
    blk_mask = [(lane // c_len) == h for h in range(HEADS_PER_TILE)]
    col_in_blk = lane % c_len
    strict = col_in_blk < rowc
    incl = col_in_blk <= rowc
    eye = jnp.where(col_in_blk == rowc, 1.0, 0.0)
    br = lax.broadcasted_iota(jnp.int32, (MXU_DIM, MXU_DIM), 0) // HEAD_DIM
    bc = lax.broadcasted_iota(jnp.int32, (MXU_DIM, MXU_DIM), 1) // HEAD_DIM
    bd_mask = br == bc

    def stack(x, masks):
        return jnp.concatenate([jnp.where(m, x, 0.0) for m in masks], axis=0).astype(BF16)

    def rows(*xs):
        return jnp.concatenate(xs, axis=0)

    units = []
    for s in range(n_seq):
        lw = lw_ref[s]
        cum = lw
        for sh in (1, 2, 4, 8, 16, 32):
            cum = cum + jnp.where(trow >= sh, pltpu.roll(cum, sh, 0), 0.0)
        ref = cum[c_len // 2 - 1:c_len // 2, :]
        e_fwd = jnp.exp(cum - ref)
        e_prev = jnp.exp(cum - lw - ref)
        e_bwd = jnp.exp(ref - cum)
        e_ref = jnp.exp(ref)
        lam_end = e_fwd[c_len - 1:c_len, :]
        lam_c = jnp.exp(cum[c_len - 1:c_len, :])
        at = -kkn_ref[s].astype(F32) * e_prev
        rt = r_ref[s].astype(F32) * e_fwd
        kt = k_ref[s].astype(F32) * e_bwd
        bt = b_ref[s].astype(F32) * e_bwd
        v = v_ref[s].astype(F32)
        for q in range(n_tiles):
            sl = slice(q * MXU_DIM, (q + 1) * MXU_DIM)
            units.append(dict(
                s=s, sl=sl, zi=s * n_tiles + q, at=at[:, sl], rt=rt[:, sl], kt=kt[:, sl],
                bt=bt[:, sl], v=v[:, sl], e_ref=e_ref[:, sl], lam_end=lam_end[:, sl],
                lam_c=lam_c[:, sl]))
        yield

    for u in units:
        ar = rows(u["at"], u["rt"]).astype(BF16)
        u["sb"] = _dot_nt(ar, stack(u["bt"], head_mask))
        u["sk"] = _dot_nt(ar, stack(u["kt"], head_mask))
    yield
    for u in units:
        u["a_ab"] = jnp.where(strict, u["sb"][:c_len], 0.0)
        u["a_rb"] = jnp.where(incl, u["sb"][c_len:], 0.0).astype(BF16)
        a_ak = jnp.where(strict, u["sk"][:c_len], 0.0)
        a_rk = jnp.where(incl, u["sk"][c_len:], 0.0)
        u["xv"] = _dot(rows(a_ak, a_rk).astype(BF16), stack(u["v"], head_mask))
        u["pw"] = _dot(u["a_ab"].astype(BF16), stack(u["a_ab"], blk_mask))
        u["t_inv"] = eye + u["a_ab"]
    yield
    for lev in range(1, WKV_LOG2_CHUNK):
        for u in units:
            rhs = stack(u["pw"], blk_mask)
            if lev < WKV_LOG2_CHUNK - 1:
                prod = _dot(rows(u["pw"], u["t_inv"]).astype(BF16), rhs)
                u["pw"] = prod[:c_len]
                u["t_inv"] = u["t_inv"] + prod[c_len:]
            else:
                u["t_inv"] = u["t_inv"] + _dot(u["t_inv"].astype(BF16), rhs)
        yield
    for u in units:
        t16 = u["t_inv"].astype(BF16)
        u["wt"] = _dot(t16, stack(u["at"] * u["e_ref"], head_mask))
        u["u0"] = _dot(t16, stack(u["xv"][:c_len], head_mask))
    yield
    for u in units:
        u["qh"] = u["rt"] * u["e_ref"] + _dot(u["a_rb"], stack(u["wt"], head_mask))
        u["y0"] = _dot(u["a_rb"], stack(u["u0"], head_mask)) + u["xv"][c_len:]
        bh = (u["bt"] * u["lam_end"]).astype(BF16)
        kh = (u["kt"] * u["lam_end"]).astype(BF16)
        u["m_lr"] = _dot_tn(u["wt"].astype(BF16), bh)
        u["g"] = _dot_tn(rows(u["u0"], u["v"]).astype(BF16), rows(bh, kh))
    yield
    for u in units:
        z = z_ref[u["zi"]]
        z16 = z.astype(BF16)
        y_ref[u["s"], :, u["sl"]] = (_dot_nt(u["qh"].astype(BF16), z16) + u["y0"]).astype(BF16)
        z_ref[u["zi"]] = (z * u["lam_c"]
                          + _dot(z16, jnp.where(bd_mask, u["m_lr"], 0.0).astype(BF16))
                          + jnp.where(bd_mask, u["g"], 0.0))


def _attn_regroup(seq_len, dil, q_ref, k_ref, v_ref, qsub, ksub, vsub, lsub):
    blk = ATTN_BLOCK
    sub = seq_len // dil
    ksub[0:blk, :] = jnp.zeros((blk, ATTN_GROUP_WIDTH), BF16)
    vsub[0:blk, :] = jnp.zeros((blk, ATTN_GROUP_WIDTH), BF16)
    for src, dst, off in ((q_ref, qsub, 0), (k_ref, ksub, blk), (v_ref, vsub, blk)):
        if dil == 1:
            dst[off:off + seq_len, :] = src[...]
        else:
            for half in range(2):
                lsub[half] = src[:, half * LANES:(half + 1) * LANES].astype(F32)
            for c in range(dil):
                for r0 in range(0, sub, blk):
                    for half in range(2):
                        rows = slice(off + c * sub + r0, off + c * sub + r0 + blk)
                        dst[rows, half * LANES:(half + 1) * LANES] = (
                            lsub[half, pl.ds(c + r0 * dil, blk, stride=dil), :].astype(BF16))


def _attn_scatter(seq_len, dil, osub, lsub, o_ref, l_ref):
    blk = ATTN_BLOCK
    sub = seq_len // dil
    for src, dst in ((osub, o_ref), (lsub, l_ref)):
        for half in range(2):
            if dil == 1:
                dst[half] = src[half]
            else:
                for c in range(dil):
                    for r0 in range(0, sub, blk):
                        dst[half, pl.ds(c + r0 * dil, blk, stride=dil), :] = (
                            src[half, c * sub + r0:c * sub + r0 + blk, :])


def _attn_block_stages(it, blocks_per_sub, qsub, ksub, vsub, osub, lsub):
    blk = ATTN_BLOCK
    qi = lax.broadcasted_iota(jnp.int32, (blk, 2 * blk), 0)
    kj = lax.broadcasted_iota(jnp.int32, (blk, 2 * blk), 1)
    dist = qi + blk - kj
    bias_any = jnp.where((dist >= 0) & (dist <= blk), 0.0, NEG_INF)
    bias_first = jnp.where(kj >= blk, bias_any, NEG_INF)
    lane = lax.broadcasted_iota(jnp.int32, (blk, ATTN_GROUP_WIDTH), 1)
    head_mask = [(lane // HEAD_DIM) == h for h in range(HEADS_PER_TILE)]
    ids = [it * ATTN_UNROLL + j for j in range(ATTN_UNROLL)]
    r0s = [pl.multiple_of(i * blk, blk) for i in ids]
    ss, ps, dens, ms, os_ = [], [], [], [], []
    for i, r0 in zip(ids, r0s):
        q = qsub[pl.ds(r0, blk), :]
        qs = jnp.concatenate([jnp.where(m, q, jnp.zeros_like(q)) for m in head_mask], axis=0)
        bias = jnp.where((i % blocks_per_sub) > 0, bias_any, bias_first)
        s_ = _dot_nt(qs, ksub[pl.ds(r0, 2 * blk), :])
        ss.append((s_.reshape(HEADS_PER_TILE, blk, 2 * blk) + bias[None]).reshape(s_.shape))
    yield
    for s_ in ss:
        ms.append(jnp.max(s_, axis=-1, keepdims=True))
    yield
    for s_, m in zip(ss, ms):
        ps.append(jnp.exp(s_ - m))
    yield
    for p in ps:
        dens.append(jnp.sum(p, axis=-1, keepdims=True))
    yield
    for p, r0 in zip(ps, r0s):
        os_.append(_dot(p.astype(BF16), vsub[pl.ds(r0, 2 * blk), :]))
    yield
    for o_all, den, m, r0 in zip(os_, dens, ms, r0s):
        o_all = o_all * (1.0 / den)
        lse = m + jnp.log(den)
        o = jnp.zeros((blk, ATTN_GROUP_WIDTH), F32)
        l = jnp.zeros((blk, ATTN_GROUP_WIDTH), F32)
        for h in range(HEADS_PER_TILE):
            rows = slice(h * blk, (h + 1) * blk)
            o = jnp.where(head_mask[h], o_all[rows], o)
            l = jnp.where(head_mask[h], lse[rows], l)
        for half in range(2):
            cols = slice(half * LANES, (half + 1) * LANES)
            osub[half, pl.ds(r0, blk), :] = o[:, cols]
            lsub[half, pl.ds(r0, blk), :] = l[:, cols]
    yield


def _mix_body(seq_len, r_ref, lw_ref, k_ref, v_ref, kkn_ref, b_ref, q_ref, ak_ref, av_ref,
              y_ref, o_ref, l_ref, z_ref, qsub, ksub, vsub, osub, lsub):
    c = pl.program_id(1)
    n_blocks = seq_len // ATTN_BLOCK
    n_pairs = MIX_SEQS * ATTN_GROUPS
    pair = jnp.minimum(c // MIX_STEPS_PER_PAIR, n_pairs - 1)
    live = c < n_pairs * MIX_STEPS_PER_PAIR
    step = c % MIX_STEPS_PER_PAIR
    g = pair % ATTN_GROUPS

    for gi, (_, dil) in enumerate(ATTN_PAIRS):
        pl.when(live & (step == 0) & (g == gi))(functools.partial(
            _attn_regroup, seq_len, dil, q_ref, ak_ref, av_ref, qsub, ksub, vsub, lsub))

    blocks_per_sub = jnp.int32(n_blocks)
    for gi, (_, dil) in enumerate(ATTN_PAIRS):
        blocks_per_sub = jnp.where(g == gi, n_blocks // dil, blocks_per_sub)

    wkv = _wkv_stages(r_ref, lw_ref, k_ref, v_ref, kkn_ref, b_ref, y_ref, z_ref)
    att = _attn_block_stages(step, blocks_per_sub, qsub, ksub, vsub, osub, lsub)
    for n, _ in enumerate(wkv):
        if n >= MIX_SEQS:
            next(att, None)
    for _ in att:
        pass

    for gi, (_, dil) in enumerate(ATTN_PAIRS):
        pl.when(live & (step == MIX_STEPS_PER_PAIR - 1) & (g == gi))(functools.partial(
            _attn_scatter, seq_len, dil, osub, lsub, o_ref, l_ref))


def _mix(batch, seq_len, r, lw, k, v, kkn, b, aq, ak, av):
    n_chunks = seq_len // WKV_CHUNK
    n_seq = MIX_SEQS
    n_pairs = n_seq * ATTN_GROUPS
    assert batch % n_seq == 0 and n_pairs * MIX_STEPS_PER_PAIR <= n_chunks
    assert MIX_STEPS_PER_PAIR * ATTN_UNROLL * ATTN_BLOCK == seq_len
    view = lambda a: a.reshape(batch, seq_len, RWKV_WIDTH)
    aview = lambda a: a.reshape(batch, seq_len, ATTN_WIDTH)
    blk = pl.BlockSpec((n_seq, WKV_CHUNK, RWKV_WIDTH), lambda bi, ci: (bi, ci, 0))

    def pair_of(ci):
        return jnp.minimum(ci // MIX_STEPS_PER_PAIR, n_pairs - 1)

    a_in = pl.BlockSpec(
        (None, seq_len, ATTN_GROUP_WIDTH),
        lambda bi, ci: (bi * n_seq + pair_of(ci) // ATTN_GROUPS, 0, pair_of(ci) % ATTN_GROUPS),
        pipeline_mode=pl.Buffered(1))
    n_slabs = ATTN_WIDTH // LANES
    a_out = pl.BlockSpec(
        (2, None, seq_len, LANES),
        lambda bi, ci: (pair_of(ci) % ATTN_GROUPS, bi * n_seq + pair_of(ci) // ATTN_GROUPS, 0, 0),
        pipeline_mode=pl.Buffered(1))
    a_shape = jax.ShapeDtypeStruct((n_slabs, batch, seq_len, LANES), F32)
    sub_rows = seq_len + ATTN_BLOCK
    y, o, l = pl.pallas_call(
        functools.partial(_mix_body, seq_len),
        grid=(batch // n_seq, n_chunks),
        in_specs=[blk] * 6 + [a_in] * 3,
        out_specs=[blk, a_out, a_out],
        out_shape=[jax.ShapeDtypeStruct((batch, seq_len, RWKV_WIDTH), BF16), a_shape, a_shape],
        scratch_shapes=[
            pltpu.VMEM((n_seq * RWKV_WIDTH // MXU_DIM, MXU_DIM, MXU_DIM), F32),
            pltpu.VMEM((seq_len, ATTN_GROUP_WIDTH), BF16),
            pltpu.VMEM((sub_rows, ATTN_GROUP_WIDTH), BF16),
            pltpu.VMEM((sub_rows, ATTN_GROUP_WIDTH), BF16),
            pltpu.VMEM((2, seq_len, LANES), F32),
            pltpu.VMEM((2, seq_len, LANES), F32),
        ],
        compiler_params=pltpu.CompilerParams(
            dimension_semantics=("arbitrary", "arbitrary"), vmem_limit_bytes=MIX_VMEM_LIMIT_BYTES),
        name="mix",
    )(*(view(a) for a in (r, lw, k, v, kkn, b)), aview(aq), aview(ak), aview(av))
    return (y.reshape(batch * seq_len, RWKV_WIDTH),
            o.reshape(n_slabs, batch * seq_len, LANES), l.reshape(n_slabs, batch * seq_len, LANES))


def _merge_body(x_ref, wkv_ref, bonus_ref, g_ref, gate_ref, o_ref, l_ref,
                lnw_ref, lnb_ref, ones_ref, wpr_ref, wpa_ref, wo_ref, out_ref):
    ones_bd = ones_ref[...]
    inv_n = 1.0 / HEAD_DIM
    wkv = wkv_ref[...].astype(F32)
    dev = wkv - _head_sums(wkv, ones_bd) * inv_n
    var = _head_sums(dev * dev, ones_bd) * inv_n
    y = dev * lax.rsqrt(var + GN_EPS) * lnw_ref[...] + lnb_ref[...]
    y_a = ((y + bonus_ref[...].astype(F32)) * g_ref[...].astype(F32)).astype(BF16)
    pa = _dot(y_a, wpr_ref[...])
    gw = ATTN_GROUP_WIDTH
    slab = lambda ref, gi: jnp.concatenate([ref[2 * gi], ref[2 * gi + 1]], axis=-1)
    l0, l1, l2 = slab(l_ref, 0), slab(l_ref, 1), slab(l_ref, 2)
    mx = jnp.maximum(jnp.maximum(l0, l1), l2)
    e0, e1, e2 = jnp.exp(l0 - mx), jnp.exp(l1 - mx), jnp.exp(l2 - mx)
    inv = 1.0 / (e0 + e1 + e2)
    pb = None
    for gi, e in enumerate((e0, e1, e2)):
        yb = (slab(o_ref, gi) * (e * inv)).astype(BF16)
        part = _dot(yb, wpa_ref[gi * gw:(gi + 1) * gw, :])
        pb = part if pb is None else pb + part
    gates = gate_ref[...].astype(F32)
    merged = gates[:, :D_MODEL] * pa + gates[:, D_MODEL:] * pb
    out_ref[...] = x_ref[...] + _dot(merged.astype(BF16), wo_ref[...])


def _merge(x, wkv, bonus, g, gates, attn_o, attn_l, ln_w, ln_b, ones_bd, wpr, wpa, wo):
    t = x.shape[0]
    tm = MERGE_TILE
    row = lambda n: pl.BlockSpec((tm, n), lambda i: (i, 0))
    small = (ln_w, ln_b, ones_bd, wpr, wpa, wo)
    return pl.pallas_call(
        _merge_body,
        grid=(t // tm,),
        in_specs=[row(D_MODEL)] * 4 + [row(2 * D_MODEL)]
                 + [pl.BlockSpec((ATTN_WIDTH // LANES, tm, LANES), lambda i: (0, i, 0))] * 2
                 + [_resident(a.shape) for a in small],
        out_specs=row(D_MODEL),
        out_shape=jax.ShapeDtypeStruct((t, D_MODEL), F32),
        compiler_params=pltpu.CompilerParams(
            dimension_semantics=("parallel",), vmem_limit_bytes=VMEM_LIMIT_BYTES),
        name="merge",
    )(x, wkv, bonus, g, gates, attn_o, attn_l, *small)


def _pad_rows(a, n):
    return jnp.pad(a, ((0, n - a.shape[0]), (0, 0)))


def _layer(x, seq_len, ffn1_norm, ffn1_w_in, ffn1_w_out, mix_norm, w_in, b_gate, rwkv_mu,
           rwkv_w0, rwkv_w2, rwkv_a0, rwkv_a2, rwkv_g2, rwkv_k_k, rwkv_k_a, rwkv_r_k,
           rwkv_ln_w, rwkv_ln_b, attn_q_norm, attn_k_norm, w_proj_rwkv, w_proj_attn,
           w_out, ffn2_norm, ffn2_w_in, ffn2_w_out):
    batch = x.shape[0] // seq_len
    row = lambda a: a.reshape(1, -1)
    w3 = 3 * RWKV_WIDTH
    lora_cols = DECAY_LORA + AAA_LORA

    zeros = lambda n: jnp.zeros((D_MODEL, n), w_in.dtype)
    w_pad = jnp.concatenate([
        w_in[:, :w3 + lora_cols + GATE_LORA], zeros(GATE_LORA_PAD - GATE_LORA),
        w_in[:, RWKV_COLS:]], axis=1).astype(BF16)
    mu_pad = jnp.pad(rwkv_mu, (0, RW_PAD - RWKV_COLS)).reshape(1, RW_PAD)
    w2_pad = _pad_rows(rwkv_w2, LORA_PAD).astype(BF16)
    a2_pad = jnp.pad(rwkv_a2, ((DECAY_LORA, 0), (0, 0))).astype(BF16)
    g2_pad = _pad_rows(rwkv_g2, GATE_LORA_PAD).astype(BF16)
    n_heads = ATTN_WIDTH // HEAD_DIM
    q_gain = row(jnp.tile(attn_q_norm, n_heads) * HEAD_DIM ** -0.5)
    k_gain = row(jnp.tile(attn_k_norm, n_heads))
    idx = jnp.arange(MXU_DIM) // HEAD_DIM
    ones_bd = (idx[:, None] == idx[None, :]).astype(BF16)

    x = _ffn(x, row(ffn1_norm), ffn1_w_in.astype(BF16), ffn1_w_out.astype(BF16))
    (r, lw, k, v, kkn, b, bonus, g, aq, ak, av, gates) = _proj(
        x, seq_len, row(mix_norm), w_pad, row(b_gate), mu_pad, row(rwkv_w0), w2_pad,
        row(rwkv_a0), a2_pad, g2_pad, row(rwkv_k_k), row(rwkv_k_a), row(rwkv_r_k),
        q_gain, k_gain, ones_bd)
    wkv, attn_o, attn_l = _mix(batch, seq_len, r, lw, k, v, kkn, b, aq, ak, av)
    x = _merge(x, wkv, bonus, g, gates, attn_o, attn_l, row(rwkv_ln_w), row(rwkv_ln_b), ones_bd,
               w_proj_rwkv.astype(BF16), w_proj_attn.astype(BF16), w_out.astype(BF16))
    return _ffn(x, row(ffn2_norm), ffn2_w_in.astype(BF16), ffn2_w_out.astype(BF16))


def kernel(x, ffn1_norm, ffn1_w_in, ffn1_w_out, mix_norm, w_in, b_gate, rwkv_mu, rwkv_w0, rwkv_w2, rwkv_a0, rwkv_a2, rwkv_g2, rwkv_k_k, rwkv_k_a, rwkv_r_k, rwkv_ln_w, rwkv_ln_b, attn_q_norm, attn_k_norm, w_proj_rwkv, w_proj_attn, w_out, ffn2_norm, ffn2_w_in, ffn2_w_out):
    batch, seq_len, d = x.shape
    params = (ffn1_norm, ffn1_w_in, ffn1_w_out, mix_norm, w_in, b_gate, rwkv_mu, rwkv_w0, rwkv_w2,
              rwkv_a0, rwkv_a2, rwkv_g2, rwkv_k_k, rwkv_k_a, rwkv_r_k, rwkv_ln_w, rwkv_ln_b,
              attn_q_norm, attn_k_norm, w_proj_rwkv, w_proj_attn, w_out, ffn2_norm, ffn2_w_in,
              ffn2_w_out)
    h = x.reshape(batch * seq_len, d)
    for layer in range(ffn1_norm.shape[0]):
        h = _layer(h, seq_len, *(p[layer] for p in params))
    return h.reshape(batch, seq_len, d)
```

```python
import functools

import jax
import jax.numpy as jnp
from jax import lax
from jax.experimental import pallas as pl
from jax.experimental.pallas import tpu as pltpu

F32 = jnp.float32
BF16 = jnp.bfloat16

D_MODEL = 1024
HEAD_DIM = 64
RWKV_WIDTH = 1024
DECAY_LORA = 64
AAA_LORA = 64
GATE_LORA = 160
RWKV_COLS = 3 * RWKV_WIDTH + DECAY_LORA + AAA_LORA + GATE_LORA
GN_EPS = 64e-5
ATTN_PAIRS = ((128, 1), (512, 4), (2048, 16))
ATTN_GROUPS = 3
ATTN_GROUP_WIDTH = 256
ATTN_WIDTH = ATTN_GROUPS * ATTN_GROUP_WIDTH
ATTN_BLOCK = 128
D_FF = 2816
RMS_EPS = 1e-6
NEG_INF = -1e30

LANES = 128
SUBLANES = 8
MXU_DIM = 256
HEADS_PER_TILE = MXU_DIM // HEAD_DIM
VMEM_LIMIT_BYTES = 56 * 1024 * 1024

LORA_PAD = 128
GATE_LORA_PAD = 256
RW_PAD = 3 * RWKV_WIDTH + LORA_PAD + GATE_LORA_PAD
ATT0 = RW_PAD
GATE0 = ATT0 + 3 * ATTN_WIDTH
IN_PAD = GATE0 + 2 * D_MODEL

FF_CHUNK = MXU_DIM
FFN_TILE = 512
PROJ_TILE = 256
MERGE_TILE = 512
WKV_CHUNK = 64
WKV_LOG2_CHUNK = 6
ATTN_UNROLL = 4
WKV_SEQS = 4


def _dot(a, b):
    return jnp.dot(a, b, preferred_element_type=F32)


def _dot_nt(a, b):
    return lax.dot_general(a, b, (((1,), (1,)), ((), ())), preferred_element_type=F32)


def _dot_tn(a, b):
    return lax.dot_general(a, b, (((0,), (0,)), ((), ())), preferred_element_type=F32)


def _rms_norm_bf16(x, gain):
    ms = jnp.mean(x * x, axis=-1, keepdims=True)
    return (x * lax.rsqrt(ms + RMS_EPS) * gain).astype(BF16)


def _head_sums(t, ones_bd):
    parts = []
    for q in range(t.shape[-1] // MXU_DIM):
        parts.append(_dot(t[:, q * MXU_DIM:(q + 1) * MXU_DIM].astype(BF16), ones_bd))
    return jnp.concatenate(parts, axis=-1)


def _resident(shape):
    return pl.BlockSpec(shape, lambda *_: (0,) * len(shape), pipeline_mode=pl.Buffered(1))


def _ffn_body(x_ref, gain_ref, w_in_ref, w_out_ref, o_ref):
    x = x_ref[...]
    h = _rms_norm_bf16(x, gain_ref[...])
    acc = None
    for c in range(D_FF // FF_CHUNK):
        lo = c * FF_CHUNK
        gate = _dot(h, w_in_ref[:, lo:lo + FF_CHUNK])
        up = _dot(h, w_in_ref[:, D_FF + lo:D_FF + lo + FF_CHUNK])
        act = (gate * jax.nn.sigmoid(gate) * up).astype(BF16)
        part = _dot(act, w_out_ref[lo:lo + FF_CHUNK, :])
        acc = part if acc is None else acc + part
    o_ref[...] = x + 0.5 * acc


def _ffn(x, gain, w_in, w_out):
    t = x.shape[0]
    tm = FFN_TILE
    return pl.pallas_call(
        _ffn_body,
        grid=(t // tm,),
        in_specs=[
            pl.BlockSpec((tm, D_MODEL), lambda i: (i, 0)),
            _resident((1, D_MODEL)),
            _resident((D_MODEL, 2 * D_FF)),
            _resident((D_FF, D_MODEL)),
        ],
        out_specs=pl.BlockSpec((tm, D_MODEL), lambda i: (i, 0)),
        out_shape=jax.ShapeDtypeStruct((t, D_MODEL), F32),
        compiler_params=pltpu.CompilerParams(
            dimension_semantics=("parallel",), vmem_limit_bytes=VMEM_LIMIT_BYTES),
        name="ffn",
    )(x, gain, w_in, w_out)


def _softplus(z):
    return jnp.maximum(z, 0.0) + jnp.log(1.0 + jnp.exp(-jnp.abs(z)))


def _proj_body(tiles_per_seq,
               x_ref, gain_ref, w_ref, bg_ref, mu_ref, w0_ref, w2_ref, a0_ref, a2_ref, g2_ref,
               kk_ref, ka_ref, rk_ref, qg_ref, kg_ref, ones_ref,
               r_o, lw_o, k_o, v_o, kkn_o, b_o, bonus_o, g_o, aq_o, ak_o, av_o, gate_o,
               carry_ref):
    i = pl.program_id(0)
    tm = x_ref.shape[0]
    w = RWKV_WIDTH
    sw = MXU_DIM
    h = _rms_norm_bf16(x_ref[...], gain_ref[...])
    ones_bd = ones_ref[...]
    row8 = lax.broadcasted_iota(jnp.int32, (SUBLANES, 1), 0)
    inv_n = 1.0 / HEAD_DIM

    @pl.when((i % tiles_per_seq) == 0)
    def _():
        carry_ref[...] = jnp.zeros_like(carry_ref)

    def mm(lo, hi):
        return _dot(h, w_ref[:, lo:hi])

    def shift(p, lo, hi):
        rolled = pltpu.roll(p, 1, 0)
        head = jnp.where(row8 == 0, carry_ref[:, lo:hi], rolled[:SUBLANES])
        prev = jnp.concatenate([head, rolled[SUBLANES:]], axis=0)
        carry_ref[:, lo:hi] = p[tm - 1:tm, :]
        return p + (prev - p) * mu_ref[:, lo:hi]

    def head_sum(t):
        return _dot(t.astype(BF16), ones_bd)

    def rwkv_mm(q):
        c = q * sw
        return [mm(j * w + c, j * w + c + sw) for j in range(3)]

    def rwkv_strip(q, prk, lora16, tanh16, sg16):
        c = q * sw
        cols = slice(c, c + sw)
        r = shift(prk[0], c, c + sw)
        k0 = shift(prk[1], w + c, w + c + sw)
        v = shift(prk[2], 2 * w + c, 2 * w + c + sw)
        wraw = w0_ref[:, cols] + _dot(tanh16, w2_ref[:, cols])
        lw_o[:, cols] = -jnp.exp(-_softplus(-wraw) - 0.5)
        a = jax.nn.sigmoid(a0_ref[:, cols] + _dot(lora16, a2_ref[:, cols]))
        g_o[:, cols] = _dot(sg16, g2_ref[:, cols]).astype(BF16)
        kk = k0 * kk_ref[:, cols]
        kkn = kk * lax.rsqrt(jnp.maximum(head_sum(kk * kk), 1e-24))
        k = k0 * (1.0 + (a - 1.0) * ka_ref[:, cols])
        r_o[:, cols] = r.astype(BF16)
        k_o[:, cols] = k.astype(BF16)
        v_o[:, cols] = v.astype(BF16)
        kkn_o[:, cols] = kkn.astype(BF16)
        b_o[:, cols] = (kkn * a).astype(BF16)
        bonus_o[:, cols] = (head_sum(r * k * rk_ref[:, cols]) * v).astype(BF16)

    def attn_mm(g):
        c = ATT0 + g * sw
        return [mm(c + j * ATTN_WIDTH, c + j * ATTN_WIDTH + sw) for j in range(3)]

    def attn_strip(g, pqkv):
        cols = slice(g * sw, (g + 1) * sw)
        pq, pk, pv = pqkv
        qn = pq * lax.rsqrt(head_sum(pq * pq) * inv_n + RMS_EPS) * qg_ref[:, cols]
        kn = pk * lax.rsqrt(head_sum(pk * pk) * inv_n + RMS_EPS) * kg_ref[:, cols]
        aq_o[:, cols] = qn.astype(BF16)
        ak_o[:, cols] = kn.astype(BF16)
        av_o[:, cols] = pv.astype(BF16)

    def gate_mm(q):
        return mm(GATE0 + q * 2 * sw, GATE0 + (q + 1) * 2 * sw)

    def gate_strip(q, pg):
        cols = slice(q * 2 * sw, (q + 1) * 2 * sw)
        gate_o[:, cols] = jax.nn.sigmoid(pg + bg_ref[:, cols]).astype(BF16)

    p_l = mm(3 * w, RW_PAD)
    nxt = rwkv_mm(0)
    ps_l = shift(p_l, 3 * w, RW_PAD)
    lora = ps_l[:, :LORA_PAD]
    lora16 = lora.astype(BF16)
    tanh16 = jnp.tanh(lora).astype(BF16)
    sg16 = jax.nn.sigmoid(ps_l[:, LORA_PAD:]).astype(BF16)

    n_rw = w // sw
    n_gate = 2 * D_MODEL // (2 * sw)
    for q in range(n_rw):
        cur = nxt
        nxt = rwkv_mm(q + 1) if q + 1 < n_rw else attn_mm(0)
        rwkv_strip(q, cur, lora16, tanh16, sg16)
    for g in range(ATTN_GROUPS):
        cur = nxt
        nxt = attn_mm(g + 1) if g + 1 < ATTN_GROUPS else gate_mm(0)
        attn_strip(g, cur)
    for q in range(n_gate):
        cur = nxt
        nxt = gate_mm(q + 1) if q + 1 < n_gate else None
        gate_strip(q, cur)


def _proj(x, seq_len, gain, w, bg, mu, w0, w2, a0, a2, g2, k_k, k_a, r_k, qg, kg, ones_bd):
    t = x.shape[0]
    tm = PROJ_TILE
    row = lambda n: pl.BlockSpec((tm, n), lambda i: (i, 0))
    wide = jax.ShapeDtypeStruct((t, RWKV_WIDTH), BF16)
    wide32 = jax.ShapeDtypeStruct((t, RWKV_WIDTH), F32)
    att = jax.ShapeDtypeStruct((t, ATTN_WIDTH), BF16)
    return pl.pallas_call(
        functools.partial(_proj_body, seq_len // tm),
        grid=(t // tm,),
        in_specs=[row(D_MODEL)] + [_resident(a.shape) for a in
                                   (gain, w, bg, mu, w0, w2, a0, a2, g2, k_k, k_a, r_k, qg, kg, ones_bd)],
        out_specs=[row(RWKV_WIDTH)] * 8 + [row(ATTN_WIDTH)] * 3 + [row(2 * D_MODEL)],
        out_shape=[wide, wide32] + [wide] * 6 + [att] * 3
                  + [jax.ShapeDtypeStruct((t, 2 * D_MODEL), BF16)],
        scratch_shapes=[pltpu.VMEM((1, RW_PAD), F32)],
        compiler_params=pltpu.CompilerParams(
            dimension_semantics=("arbitrary",), vmem_limit_bytes=VMEM_LIMIT_BYTES),
        name="proj",
    )(x, gain, w, bg, mu, w0, w2, a0, a2, g2, k_k, k_a, r_k, qg, kg, ones_bd)


def _wkv_stages(r_ref, lw_ref, k_ref, v_ref, kkn_ref, b_ref, y_ref, z_ref):
    c_len = WKV_CHUNK
    n_seq = r_ref.shape[0]
    n_tiles = RWKV_WIDTH // MXU_DIM

    @pl.when(pl.program_id(1) == 0)
    def _():
        z_ref[...] = jnp.zeros_like(z_ref)

    trow = lax.broadcasted_iota(jnp.int32, (c_len, 1), 0)
    lane = lax.broadcasted_iota(jnp.int32, (c_len, MXU_DIM), 1)
    rowc = lax.broadcasted_iota(jnp.int32, (c_len, MXU_DIM), 0)
    head_mask = [(lane // HEAD_DIM) == h for h in range(HEADS_PER_TILE)]
    blk_mask = [(lane // c_len) == h for h in range(HEADS_PER_TILE)]
    col_in_blk = lane % c_len
    strict = col_in_blk < rowc
    incl = col_in_blk <= rowc
    eye = jnp.where(col_in_blk == rowc, 1.0, 0.0)
    br = lax.broadcasted_iota(jnp.int32, (MXU_DIM, MXU_DIM), 0) // HEAD_DIM
    bc = lax.broadcasted_iota(jnp.int32, (MXU_DIM, MXU_DIM), 1) // HEAD_DIM
    bd_mask = br == bc

    def stack(x, masks):
        return jnp.concatenate([jnp.where(m, x, 0.0) for m in masks], axis=0).astype(BF16)

    def rows(*xs):
        return jnp.concatenate(xs, axis=0)

    units = []
    for s in range(n_seq):
        lw = lw_ref[s]
        cum = lw
        for sh in (1, 2, 4, 8, 16, 32):
            cum = cum + jnp.where(trow >= sh, pltpu.roll(cum, sh, 0), 0.0)
        ref = cum[c_len // 2 - 1:c_len // 2, :]
        e_fwd = jnp.exp(cum - ref)
        e_prev = jnp.exp(cum - lw - ref)
        e_bwd = jnp.exp(ref - cum)
        e_ref = jnp.exp(ref)
        lam_end = e_fwd[c_len - 1:c_len, :]
        lam_c = jnp.exp(cum[c_len - 1:c_len, :])
        at = -kkn_ref[s].astype(F32) * e_prev
        rt = r_ref[s].astype(F32) * e_fwd
        kt = k_ref[s].astype(F32) * e_bwd
        bt = b_ref[s].astype(F32) * e_bwd
        v = v_ref[s].astype(F32)
        for q in range(n_tiles):
            sl = slice(q * MXU_DIM, (q + 1) * MXU_DIM)
            units.append(dict(
                s=s, sl=sl, zi=s * n_tiles + q, at=at[:, sl], rt=rt[:, sl], kt=kt[:, sl],
                bt=bt[:, sl], v=v[:, sl], e_ref=e_ref[:, sl], lam_end=lam_end[:, sl],
                lam_c=lam_c[:, sl]))
        yield

    for u in units:
        ar = rows(u["at"], u["rt"]).astype(BF16)
        u["sb"] = _dot_nt(ar, stack(u["bt"], head_mask))
        u["sk"] = _dot_nt(ar, stack(u["kt"], head_mask))
    yield
    for u in units:
        u["a_ab"] = jnp.where(strict, u["sb"][:c_len], 0.0)
        u["a_rb"] = jnp.where(incl, u["sb"][c_len:], 0.0).astype(BF16)
        a_ak = jnp.where(strict, u["sk"][:c_len], 0.0)
        a_rk = jnp.where(incl, u["sk"][c_len:], 0.0)
        u["xv"] = _dot(rows(a_ak, a_rk).astype(BF16), stack(u["v"], head_mask))
        u["pw"] = _dot(u["a_ab"].astype(BF16), stack(u["a_ab"], blk_mask))
        u["t_inv"] = eye + u["a_ab"]
    yield
    for lev in range(1, WKV_LOG2_CHUNK):
        for u in units:
            rhs = stack(u["pw"], blk_mask)
            if lev < WKV_LOG2_CHUNK - 1:
                prod = _dot(rows(u["pw"], u["t_inv"]).astype(BF16), rhs)
                u["pw"] = prod[:c_len]
                u["t_inv"] = u["t_inv"] + prod[c_len:]
            else:
                u["t_inv"] = u["t_inv"] + _dot(u["t_inv"].astype(BF16), rhs)
        yield
    for u in units:
        t16 = u["t_inv"].astype(BF16)
        u["wt"] = _dot(t16, stack(u["at"] * u["e_ref"], head_mask))
        u["u0"] = _dot(t16, stack(u["xv"][:c_len], head_mask))
    yield
    for u in units:
        u["qh"] = u["rt"] * u["e_ref"] + _dot(u["a_rb"], stack(u["wt"], head_mask))
        u["y0"] = _dot(u["a_rb"], stack(u["u0"], head_mask)) + u["xv"][c_len:]
        bh = (u["bt"] * u["lam_end"]).astype(BF16)
        kh = (u["kt"] * u["lam_end"]).astype(BF16)
        u["m_lr"] = _dot_tn(u["wt"].astype(BF16), bh)
        u["g"] = _dot_tn(rows(u["u0"], u["v"]).astype(BF16), rows(bh, kh))
    yield
    for u in units:
        z = z_ref[u["zi"]]
        z16 = z.astype(BF16)
        y_ref[u["s"], :, u["sl"]] = (_dot_nt(u["qh"].astype(BF16), z16) + u["y0"]).astype(BF16)
        z_ref[u["zi"]] = (z * u["lam_c"]
                          + _dot(z16, jnp.where(bd_mask, u["m_lr"], 0.0).astype(BF16))
                          + jnp.where(bd_mask, u["g"], 0.0))


def _wkv_body(*refs):
    for _ in _wkv_stages(*refs):
        pass


def _wkv(batch, seq_len, r, lw, k, v, kkn, b):
    n_chunks = seq_len // WKV_CHUNK
    n_seq = WKV_SEQS if batch % WKV_SEQS == 0 else 1
    view = lambda a: a.reshape(batch, seq_len, RWKV_WIDTH)
    blk = pl.BlockSpec((n_seq, WKV_CHUNK, RWKV_WIDTH), lambda bi, ci: (bi, ci, 0))
    y = pl.pallas_call(
        _wkv_body,
        grid=(batch // n_seq, n_chunks),
        in_specs=[blk] * 6,
        out_specs=blk,
        out_shape=jax.ShapeDtypeStruct((batch, seq_len, RWKV_WIDTH), BF16),
        scratch_shapes=[pltpu.VMEM((n_seq * RWKV_WIDTH // MXU_DIM, MXU_DIM, MXU_DIM), F32)],
        compiler_params=pltpu.CompilerParams(
            dimension_semantics=("arbitrary", "arbitrary"), vmem_limit_bytes=VMEM_LIMIT_BYTES),
        name="wkv",
    )(*(view(a) for a in (r, lw, k, v, kkn, b)))
    return y.reshape(batch * seq_len, RWKV_WIDTH)


def _phase_perm(dil):
    n = MXU_DIM // dil
    out = lax.broadcasted_iota(jnp.int32, (MXU_DIM, MXU_DIM), 0)
    src = lax.broadcasted_iota(jnp.int32, (MXU_DIM, MXU_DIM), 1)
    return jnp.where(src == (out % n) * dil + out // n, 1.0, 0.0).astype(BF16)


def _attn_regroup(seq_len, dil, q_ref, k_ref, v_ref, qsub, ksub, vsub):
    blk = ATTN_BLOCK
    sub = seq_len // dil
    n = MXU_DIM // dil
    ksub[0:blk, :] = jnp.zeros((blk, ATTN_GROUP_WIDTH), BF16)
    vsub[0:blk, :] = jnp.zeros((blk, ATTN_GROUP_WIDTH), BF16)
    perm = None if dil == 1 else _phase_perm(dil)
    for src, dst, off in ((q_ref, qsub, 0), (k_ref, ksub, blk), (v_ref, vsub, blk)):
        if dil == 1:
            dst[off:off + seq_len, :] = src[...]
            continue
        for t in range(seq_len // MXU_DIM):
            y = _dot(perm, src[t * MXU_DIM:(t + 1) * MXU_DIM, :]).astype(BF16)
            for c in range(dil):
                dst[off + c * sub + t * n:off + c * sub + (t + 1) * n, :] = y[c * n:(c + 1) * n]


def _attn_scatter(seq_len, dil, osub, lsub, o_ref, l_ref):
    sub = seq_len // dil
    n = MXU_DIM // dil
    if dil == 1:
        o_ref[...] = osub[...]
        l_ref[...] = lsub[...]
        return
    perm = _phase_perm(dil)
    for t in range(seq_len // MXU_DIM):
        rows = lambda ref: jnp.concatenate(
            [ref[c * sub + t * n:c * sub + (t + 1) * n, :] for c in range(dil)], axis=0)
        out = slice(t * MXU_DIM, (t + 1) * MXU_DIM)
        o_ref[out, :] = _dot_tn(perm, rows(osub)).astype(BF16)
        l = rows(lsub)
        hi = l.astype(BF16)
        rem = l - hi.astype(F32)
        mid = rem.astype(BF16)
        lo = (rem - mid.astype(F32)).astype(BF16)
        l_ref[out, :] = _dot_tn(perm, hi) + _dot_tn(perm, mid) + _dot_tn(perm, lo)


def _attn_body(seq_len, q_ref, k_ref, v_ref, o_ref, l_ref, qsub, ksub, vsub, osub, lsub):
    g = pl.program_id(1)
    blk = ATTN_BLOCK
    n_blocks = seq_len // blk

    for gi, (_, dil) in enumerate(ATTN_PAIRS):
        pl.when(g == gi)(functools.partial(
            _attn_regroup, seq_len, dil, q_ref, k_ref, v_ref, qsub, ksub, vsub))

    blocks_per_sub = jnp.int32(n_blocks)
    for gi, (_, dil) in enumerate(ATTN_PAIRS):
        blocks_per_sub = jnp.where(g == gi, n_blocks // dil, blocks_per_sub)

    qi = lax.broadcasted_iota(jnp.int32, (blk, 2 * blk), 0)
    kj = lax.broadcasted_iota(jnp.int32, (blk, 2 * blk), 1)
    dist = qi + blk - kj
    bias_any = jnp.where((dist >= 0) & (dist <= blk), 0.0, NEG_INF)
    bias_first = jnp.where(kj >= blk, bias_any, NEG_INF)
    lane = lax.broadcasted_iota(jnp.int32, (blk, ATTN_GROUP_WIDTH), 1)
    head_mask = [(lane // HEAD_DIM) == h for h in range(HEADS_PER_TILE)]

    def blocks(it, carry):
        ids = [it * ATTN_UNROLL + j for j in range(ATTN_UNROLL)]
        r0s = [pl.multiple_of(i * blk, blk) for i in ids]
        ss, ps, dens, ms, os_ = [], [], [], [], []
        for i, r0 in zip(ids, r0s):
            q = qsub[pl.ds(r0, blk), :]
            qs = jnp.concatenate([jnp.where(m, q, jnp.zeros_like(q)) for m in head_mask], axis=0)
            bias = jnp.where((i % blocks_per_sub) > 0, bias_any, bias_first)
            s_ = _dot_nt(qs, ksub[pl.ds(r0, 2 * blk), :])
            ss.append((s_.reshape(HEADS_PER_TILE, blk, 2 * blk) + bias[None]).reshape(s_.shape))
        for s_ in ss:
            ms.append(jnp.max(s_, axis=-1, keepdims=True))
        for s_, m in zip(ss, ms):
            ps.append(jnp.exp(s_ - m))
        for p in ps:
            dens.append(jnp.sum(p, axis=-1, keepdims=True))
        for p, r0 in zip(ps, r0s):
            os_.append(_dot(p.astype(BF16), vsub[pl.ds(r0, 2 * blk), :]))
        for o_all, den, m, r0 in zip(os_, dens, ms, r0s):
            o_all = o_all * (1.0 / den)
            lse = m + jnp.log(den)
            o = jnp.zeros((blk, ATTN_GROUP_WIDTH), F32)
            l = jnp.zeros((blk, ATTN_GROUP_WIDTH), F32)
            for h in range(HEADS_PER_TILE):
                rows = slice(h * blk, (h + 1) * blk)
                o = jnp.where(head_mask[h], o_all[rows], o)
                l = jnp.where(head_mask[h], lse[rows], l)
            osub[pl.ds(r0, blk), :] = o.astype(BF16)
            lsub[pl.ds(r0, blk), :] = l
        return carry

    lax.fori_loop(0, n_blocks // ATTN_UNROLL, blocks, 0)

    for gi, (_, dil) in enumerate(ATTN_PAIRS):
        pl.when(g == gi)(functools.partial(_attn_scatter, seq_len, dil, osub, lsub, o_ref, l_ref))


def _attn(batch, seq_len, aq, ak, av):
    view = lambda a: a.reshape(batch, seq_len, ATTN_WIDTH)
    spec = pl.BlockSpec((None, seq_len, ATTN_GROUP_WIDTH), lambda bi, g: (bi, 0, g))
    sub_rows = seq_len + ATTN_BLOCK
    o, l = pl.pallas_call(
        functools.partial(_attn_body, seq_len),
        grid=(batch, ATTN_GROUPS),
        in_specs=[spec] * 3,
        out_specs=[spec] * 2,
        out_shape=[jax.ShapeDtypeStruct((batch, seq_len, ATTN_WIDTH), BF16),
                   jax.ShapeDtypeStruct((batch, seq_len, ATTN_WIDTH), F32)],
        scratch_shapes=[
            pltpu.VMEM((seq_len, ATTN_GROUP_WIDTH), BF16),
            pltpu.VMEM((sub_rows, ATTN_GROUP_WIDTH), BF16),
            pltpu.VMEM((sub_rows, ATTN_GROUP_WIDTH), BF16),
            pltpu.VMEM((seq_len, ATTN_GROUP_WIDTH), BF16),
            pltpu.VMEM((seq_len, ATTN_GROUP_WIDTH), F32),
        ],
        compiler_params=pltpu.CompilerParams(
            dimension_semantics=("parallel", "arbitrary"), vmem_limit_bytes=VMEM_LIMIT_BYTES),
        name="attn",
    )(view(aq), view(ak), view(av))
    return o.reshape(batch * seq_len, ATTN_WIDTH), l.reshape(batch * seq_len, ATTN_WIDTH)


def _merge_body(x_ref, wkv_ref, bonus_ref, g_ref, gate_ref, o_ref, l_ref,
                lnw_ref, lnb_ref, ones_ref, wpr_ref, wpa_ref, wo_ref, out_ref):
    ones_bd = ones_ref[...]
    inv_n = 1.0 / HEAD_DIM
    wkv = wkv_ref[...].astype(F32)
    dev = wkv - _head_sums(wkv, ones_bd) * inv_n
    var = _head_sums(dev * dev, ones_bd) * inv_n
    y = dev * lax.rsqrt(var + GN_EPS) * lnw_ref[...] + lnb_ref[...]
    y_a = ((y + bonus_ref[...].astype(F32)) * g_ref[...].astype(F32)).astype(BF16)
    pa = _dot(y_a, wpr_ref[...])
    gw = ATTN_GROUP_WIDTH
    l0, l1, l2 = l_ref[:, 0:gw], l_ref[:, gw:2 * gw], l_ref[:, 2 * gw:3 * gw]
    mx = jnp.maximum(jnp.maximum(l0, l1), l2)
    e0, e1, e2 = jnp.exp(l0 - mx), jnp.exp(l1 - mx), jnp.exp(l2 - mx)
    inv = 1.0 / (e0 + e1 + e2)
    pb = None
    for gi, e in enumerate((e0, e1, e2)):
        yb = (o_ref[:, gi * gw:(gi + 1) * gw].astype(F32) * (e * inv)).astype(BF16)
        part = _dot(yb, wpa_ref[gi * gw:(gi + 1) * gw, :])
        pb = part if pb is None else pb + part
    gates = gate_ref[...].astype(F32)
    merged = gates[:, :D_MODEL] * pa + gates[:, D_MODEL:] * pb
    out_ref[...] = x_ref[...] + _dot(merged.astype(BF16), wo_ref[...])


def _merge(x, wkv, bonus, g, gates, attn_o, attn_l, ln_w, ln_b, ones_bd, wpr, wpa, wo):
    t = x.shape[0]
    tm = MERGE_TILE
    row = lambda n: pl.BlockSpec((tm, n), lambda i: (i, 0))
    small = (ln_w, ln_b, ones_bd, wpr, wpa, wo)
    return pl.pallas_call(
        _merge_body,
        grid=(t // tm,),
        in_specs=[row(D_MODEL)] * 4 + [row(2 * D_MODEL)] + [row(ATTN_WIDTH)] * 2
                 + [_resident(a.shape) for a in small],
        out_specs=row(D_MODEL),
        out_shape=jax.ShapeDtypeStruct((t, D_MODEL), F32),
        compiler_params=pltpu.CompilerParams(
            dimension_semantics=("parallel",), vmem_limit_bytes=VMEM_LIMIT_BYTES),
        name="merge",
    )(x, wkv, bonus, g, gates, attn_o, attn_l, *small)


def _pad_rows(a, n):
    return jnp.pad(a, ((0, n - a.shape[0]), (0, 0)))


def _layer(x, seq_len, ffn1_norm, ffn1_w_in, ffn1_w_out, mix_norm, w_in, b_gate, rwkv_mu,
           rwkv_w0, rwkv_w2, rwkv_a0, rwkv_a2, rwkv_g2, rwkv_k_k, rwkv_k_a, rwkv_r_k,
           rwkv_ln_w, rwkv_ln_b, attn_q_norm, attn_k_norm, w_proj_rwkv, w_proj_attn,
           w_out, ffn2_norm, ffn2_w_in, ffn2_w_out):
    batch = x.shape[0] // seq_len
    row = lambda a: a.reshape(1, -1)
    w3 = 3 * RWKV_WIDTH
    lora_cols = DECAY_LORA + AAA_LORA

    zeros = lambda n: jnp.zeros((D_MODEL, n), w_in.dtype)
    w_pad = jnp.concatenate([
        w_in[:, :w3 + lora_cols + GATE_LORA], zeros(GATE_LORA_PAD - GATE_LORA),
        w_in[:, RWKV_COLS:]], axis=1).astype(BF16)
    mu_pad = jnp.pad(rwkv_mu, (0, RW_PAD - RWKV_COLS)).reshape(1, RW_PAD)
    w2_pad = _pad_rows(rwkv_w2, LORA_PAD).astype(BF16)
    a2_pad = jnp.pad(rwkv_a2, ((DECAY_LORA, 0), (0, 0))).astype(BF16)
    g2_pad = _pad_rows(rwkv_g2, GATE_LORA_PAD).astype(BF16)
    n_heads = ATTN_WIDTH // HEAD_DIM
    q_gain = row(jnp.tile(attn_q_norm, n_heads) * HEAD_DIM ** -0.5)
    k_gain = row(jnp.tile(attn_k_norm, n_heads))
    idx = jnp.arange(MXU_DIM) // HEAD_DIM
    ones_bd = (idx[:, None] == idx[None, :]).astype(BF16)

    x = _ffn(x, row(ffn1_norm), ffn1_w_in.astype(BF16), ffn1_w_out.astype(BF16))
    (r, lw, k, v, kkn, b, bonus, g, aq, ak, av, gates) = _proj(
        x, seq_len, row(mix_norm), w_pad, row(b_gate), mu_pad, row(rwkv_w0), w2_pad,
        row(rwkv_a0), a2_pad, g2_pad, row(rwkv_k_k), row(rwkv_k_a), row(rwkv_r_k),
        q_gain, k_gain, ones_bd)
    wkv = _wkv(batch, seq_len, r, lw, k, v, kkn, b)
    attn_o, attn_l = _attn(batch, seq_len, aq, ak, av)
    x = _merge(x, wkv, bonus, g, gates, attn_o, attn_l, row(rwkv_ln_w), row(rwkv_ln_b), ones_bd,
               w_proj_rwkv.astype(BF16), w_proj_attn.astype(BF16), w_out.astype(BF16))
    return _ffn(x, row(ffn2_norm), ffn2_w_in.astype(BF16), ffn2_w_out.astype(BF16))


def kernel(x, ffn1_norm, ffn1_w_in, ffn1_w_out, mix_norm, w_in, b_gate, rwkv_mu, rwkv_w0, rwkv_w2, rwkv_a0, rwkv_a2, rwkv_g2, rwkv_k_k, rwkv_k_a, rwkv_r_k, rwkv_ln_w, rwkv_ln_b, attn_q_norm, attn_k_norm, w_proj_rwkv, w_proj_attn, w_out, ffn2_norm, ffn2_w_in, ffn2_w_out):
    batch, seq_len, d = x.shape
    params = (ffn1_norm, ffn1_w_in, ffn1_w_out, mix_norm, w_in, b_gate, rwkv_mu, rwkv_w0, rwkv_w2,
              rwkv_a0, rwkv_a2, rwkv_g2, rwkv_k_k, rwkv_k_a, rwkv_r_k, rwkv_ln_w, rwkv_ln_b,
              attn_q_norm, attn_k_norm, w_proj_rwkv, w_proj_attn, w_out, ffn2_norm, ffn2_w_in,
              ffn2_w_out)
    h = x.reshape(batch * seq_len, d)
    for layer in range(ffn1_norm.shape[0]):
        h = _layer(h, seq_len, *(p[layer] for p in params))
    return h.reshape(batch, seq_len, d)
```

```python
import functools

import jax
import jax.numpy as jnp
from jax import lax
from jax.experimental import pallas as pl
from jax.experimental.pallas import tpu as pltpu

F32 = jnp.float32
BF16 = jnp.bfloat16

D_MODEL = 1024
HEAD_DIM = 64
RWKV_WIDTH = 1024
DECAY_LORA = 64
AAA_LORA = 64
GATE_LORA = 160
RWKV_COLS = 3 * RWKV_WIDTH + DECAY_LORA + AAA_LORA + GATE_LORA
GN_EPS = 64e-5
ATTN_PAIRS = ((128, 1), (512, 4), (2048, 16))
ATTN_GROUPS = 3
ATTN_GROUP_WIDTH = 256
ATTN_WIDTH = ATTN_GROUPS * ATTN_GROUP_WIDTH
ATTN_BLOCK = 128
D_FF = 2816
RMS_EPS = 1e-6
NEG_INF = -1e30

LANES = 128
SUBLANES = 8
MXU_DIM = 256
HEADS_PER_TILE = MXU_DIM // HEAD_DIM
VMEM_LIMIT_BYTES = 56 * 1024 * 1024

LORA_PAD = 128
GATE_LORA_PAD = 256
RW_PAD = 3 * RWKV_WIDTH + LORA_PAD + GATE_LORA_PAD
ATT0 = RW_PAD
GATE0 = ATT0 + 3 * ATTN_WIDTH
IN_PAD = GATE0 + 2 * D_MODEL

FF_CHUNK = MXU_DIM
FFN_TILE = 512
PROJ_TILE = 256
MERGE_TILE = 512
WKV_CHUNK = 64
WKV_LOG2_CHUNK = 6
ATTN_UNROLL = 4
WKV_SEQS = 4


def _dot(a, b):
    return jnp.dot(a, b, preferred_element_type=F32)


def _dot_nt(a, b):
    return lax.dot_general(a, b, (((1,), (1,)), ((), ())), preferred_element_type=F32)


def _dot_tn(a, b):
    return lax.dot_general(a, b, (((0,), (0,)), ((), ())), preferred_element_type=F32)


def _rms_norm_bf16(x, gain):
    ms = jnp.mean(x * x, axis=-1, keepdims=True)
    return (x * lax.rsqrt(ms + RMS_EPS) * gain).astype(BF16)


def _head_sums(t, ones_bd):
    parts = []
    for q in range(t.shape[-1] // MXU_DIM):
        parts.append(_dot(t[:, q * MXU_DIM:(q + 1) * MXU_DIM].astype(BF16), ones_bd))
    return jnp.concatenate(parts, axis=-1)


def _resident(shape):
    return pl.BlockSpec(shape, lambda *_: (0,) * len(shape), pipeline_mode=pl.Buffered(1))


def _ffn_body(x_ref, gain_ref, w_in_ref, w_out_ref, o_ref):
    x = x_ref[...]
    h = _rms_norm_bf16(x, gain_ref[...])
    acc = None
    for c in range(D_FF // FF_CHUNK):
        lo = c * FF_CHUNK
        gate = _dot(h, w_in_ref[:, lo:lo + FF_CHUNK])
        up = _dot(h, w_in_ref[:, D_FF + lo:D_FF + lo + FF_CHUNK])
        act = (gate * jax.nn.sigmoid(gate) * up).astype(BF16)
        part = _dot(act, w_out_ref[lo:lo + FF_CHUNK, :])
        acc = part if acc is None else acc + part
    o_ref[...] = x + 0.5 * acc


def _ffn(x, gain, w_in, w_out):
    t = x.shape[0]
    tm = FFN_TILE
    return pl.pallas_call(
        _ffn_body,
        grid=(t // tm,),
        in_specs=[
            pl.BlockSpec((tm, D_MODEL), lambda i: (i, 0)),
            _resident((1, D_MODEL)),
            _resident((D_MODEL, 2 * D_FF)),
            _resident((D_FF, D_MODEL)),
        ],
        out_specs=pl.BlockSpec((tm, D_MODEL), lambda i: (i, 0)),
        out_shape=jax.ShapeDtypeStruct((t, D_MODEL), F32),
        compiler_params=pltpu.CompilerParams(
            dimension_semantics=("parallel",), vmem_limit_bytes=VMEM_LIMIT_BYTES),
        name="ffn",
    )(x, gain, w_in, w_out)


def _softplus(z):
    return jnp.maximum(z, 0.0) + jnp.log(1.0 + jnp.exp(-jnp.abs(z)))


def _proj_body(tiles_per_seq,
               x_ref, gain_ref, w_ref, wag_ref, bg_ref, mu_ref, w0_ref, w2_ref, a0_ref, a2_ref, g2_ref,
               kk_ref, ka_ref, rk_ref, qg_ref, kg_ref, ones_ref,
               r_o, lw_o, k_o, v_o, kkn_o, b_o, bonus_o, g_o, aq_o, ak_o, av_o, gate_o,
               carry_ref):
    i = pl.program_id(0)
    tm = x_ref.shape[0]
    w = RWKV_WIDTH
    sw = MXU_DIM
    h = _rms_norm_bf16(x_ref[...], gain_ref[...])
    ones_bd = ones_ref[...]
    row8 = lax.broadcasted_iota(jnp.int32, (SUBLANES, 1), 0)
    inv_n = 1.0 / HEAD_DIM

    @pl.when((i % tiles_per_seq) == 0)
    def _():
        carry_ref[...] = jnp.zeros_like(carry_ref)

    def mm(lo, hi):
        if hi <= ATT0:
            return _dot(h, w_ref[:, lo:hi])
        return _dot(h, wag_ref[:, lo - ATT0:hi - ATT0])

    def shift(p, lo, hi):
        rolled = pltpu.roll(p, 1, 0)
        head = jnp.where(row8 == 0, carry_ref[:, lo:hi], rolled[:SUBLANES])
        prev = jnp.concatenate([head, rolled[SUBLANES:]], axis=0)
        carry_ref[:, lo:hi] = p[tm - 1:tm, :]
        return p + (prev - p) * mu_ref[:, lo:hi]

    def head_sum(t):
        return _dot(t.astype(BF16), ones_bd)

    def rwkv_mm(q):
        c = q * sw
        return [mm(j * w + c, j * w + c + sw) for j in range(3)]

    def rwkv_strip(q, prk, lora16, tanh16, sg16):
        c = q * sw
        cols = slice(c, c + sw)
        r = shift(prk[0], c, c + sw)
        k0 = shift(prk[1], w + c, w + c + sw)
        v = shift(prk[2], 2 * w + c, 2 * w + c + sw)
        wraw = w0_ref[:, cols] + _dot(tanh16, w2_ref[:, cols])
        lw_o[:, cols] = -jnp.exp(-_softplus(-wraw) - 0.5)
        a = jax.nn.sigmoid(a0_ref[:, cols] + _dot(lora16, a2_ref[:, cols]))
        g_o[:, cols] = _dot(sg16, g2_ref[:, cols]).astype(BF16)
        kk = k0 * kk_ref[:, cols]
        kkn = kk * lax.rsqrt(jnp.maximum(head_sum(kk * kk), 1e-24))
        k = k0 * (1.0 + (a - 1.0) * ka_ref[:, cols])
        r_o[:, cols] = r.astype(BF16)
        k_o[:, cols] = k.astype(BF16)
        v_o[:, cols] = v.astype(BF16)
        kkn_o[:, cols] = kkn.astype(BF16)
        b_o[:, cols] = (kkn * a).astype(BF16)
        bonus_o[:, cols] = (head_sum(r * k * rk_ref[:, cols]) * v).astype(BF16)

    def attn_mm(g):
        c = ATT0 + g * sw
        return [mm(c + j * ATTN_WIDTH, c + j * ATTN_WIDTH + sw) for j in range(3)]

    def attn_strip(g, pqkv):
        cols = slice(g * sw, (g + 1) * sw)
        pq, pk, pv = pqkv
        qn = pq * lax.rsqrt(head_sum(pq * pq) * inv_n + RMS_EPS) * qg_ref[:, cols]
        kn = pk * lax.rsqrt(head_sum(pk * pk) * inv_n + RMS_EPS) * kg_ref[:, cols]
        aq_o[:, cols] = qn.astype(BF16)
        ak_o[:, cols] = kn.astype(BF16)
        av_o[:, cols] = pv.astype(BF16)

    def gate_mm(q):
        return mm(GATE0 + q * 2 * sw, GATE0 + (q + 1) * 2 * sw)

    def gate_strip(q, pg):
        cols = slice(q * 2 * sw, (q + 1) * 2 * sw)
        gate_o[:, cols] = jax.nn.sigmoid(pg + bg_ref[:, cols]).astype(BF16)

    p_l = mm(3 * w, RW_PAD)
    nxt = rwkv_mm(0)
    ps_l = shift(p_l, 3 * w, RW_PAD)
    lora = ps_l[:, :LORA_PAD]
    lora16 = lora.astype(BF16)
    tanh16 = jnp.tanh(lora).astype(BF16)
    sg16 = jax.nn.sigmoid(ps_l[:, LORA_PAD:]).astype(BF16)

    n_rw = w // sw
    n_gate = 2 * D_MODEL // (2 * sw)
    for q in range(n_rw):
        cur = nxt
        nxt = rwkv_mm(q + 1) if q + 1 < n_rw else attn_mm(0)
        rwkv_strip(q, cur, lora16, tanh16, sg16)
    for g in range(ATTN_GROUPS):
        cur = nxt
        nxt = attn_mm(g + 1) if g + 1 < ATTN_GROUPS else gate_mm(0)
        attn_strip(g, cur)
    for q in range(n_gate):
        cur = nxt
        nxt = gate_mm(q + 1) if q + 1 < n_gate else None
        gate_strip(q, cur)


def _proj(x, seq_len, gain, w, wag, bg, mu, w0, w2, a0, a2, g2, k_k, k_a, r_k, qg, kg, ones_bd):
    t = x.shape[0]
    tm = PROJ_TILE
    row = lambda n: pl.BlockSpec((tm, n), lambda i: (i, 0))
    wide = jax.ShapeDtypeStruct((t, RWKV_WIDTH), BF16)
    wide32 = jax.ShapeDtypeStruct((t, RWKV_WIDTH), F32)
    att = jax.ShapeDtypeStruct((t, ATTN_WIDTH), BF16)
    return pl.pallas_call(
        functools.partial(_proj_body, seq_len // tm),
        grid=(t // tm,),
        in_specs=[row(D_MODEL)] + [_resident(a.shape) for a in
                                   (gain, w, wag, bg, mu, w0, w2, a0, a2, g2, k_k, k_a, r_k, qg, kg, ones_bd)],
        out_specs=[row(RWKV_WIDTH)] * 8 + [row(ATTN_WIDTH)] * 3 + [row(2 * D_MODEL)],
        out_shape=[wide, wide32] + [wide] * 6 + [att] * 3
                  + [jax.ShapeDtypeStruct((t, 2 * D_MODEL), BF16)],
        scratch_shapes=[pltpu.VMEM((1, RW_PAD), F32)],
        compiler_params=pltpu.CompilerParams(
            dimension_semantics=("arbitrary",), vmem_limit_bytes=VMEM_LIMIT_BYTES),
        name="proj",
    )(x, gain, w, wag, bg, mu, w0, w2, a0, a2, g2, k_k, k_a, r_k, qg, kg, ones_bd)


def _wkv_stages(r_ref, lw_ref, k_ref, v_ref, kkn_ref, b_ref, y_ref, z_ref):
    c_len = WKV_CHUNK
    n_seq = r_ref.shape[0]
    n_tiles = RWKV_WIDTH // MXU_DIM

    @pl.when(pl.program_id(1) == 0)
    def _():
        z_ref[...] = jnp.zeros_like(z_ref)

    trow = lax.broadcasted_iota(jnp.int32, (c_len, 1), 0)
    lane = lax.broadcasted_iota(jnp.int32, (c_len, MXU_DIM), 1)
    rowc = lax.broadcasted_iota(jnp.int32, (c_len, MXU_DIM), 0)
    head_mask = [(lane // HEAD_DIM) == h for h in range(HEADS_PER_TILE)]
    blk_mask = [(lane // c_len) == h for h in range(HEADS_PER_TILE)]
    col_in_blk = lane % c_len
    strict = col_in_blk < rowc
    incl = col_in_blk <= rowc
    eye = jnp.where(col_in_blk == rowc, 1.0, 0.0)
    br = lax.broadcasted_iota(jnp.int32, (MXU_DIM, MXU_DIM), 0) // HEAD_DIM
    bc = lax.broadcasted_iota(jnp.int32, (MXU_DIM, MXU_DIM), 1) // HEAD_DIM
    bd_mask = br == bc

    def stack(x, masks):
        return jnp.concatenate([jnp.where(m, x, 0.0) for m in masks], axis=0).astype(BF16)

    def rows(*xs):
        return jnp.concatenate(xs, axis=0)

    units = []
    for s in range(n_seq):
        lw = lw_ref[s]
        cum = lw
        for sh in (1, 2, 4, 8, 16, 32):
            cum = cum + jnp.where(trow >= sh, pltpu.roll(cum, sh, 0), 0.0)
        ref = cum[c_len // 2 - 1:c_len // 2, :]
        e_fwd = jnp.exp(cum - ref)
        e_prev = jnp.exp(cum - lw - ref)
        e_bwd = jnp.exp(ref - cum)
        e_ref = jnp.exp(ref)
        lam_end = e_fwd[c_len - 1:c_len, :]
        lam_c = jnp.exp(cum[c_len - 1:c_len, :])
        at = -kkn_ref[s].astype(F32) * e_prev
        rt = r_ref[s].astype(F32) * e_fwd
        kt = k_ref[s].astype(F32) * e_bwd
        bt = b_ref[s].astype(F32) * e_bwd
        v = v_ref[s].astype(F32)
        for q in range(n_tiles):
            sl = slice(q * MXU_DIM, (q + 1) * MXU_DIM)
            units.append(dict(
                s=s, sl=sl, zi=s * n_tiles + q, at=at[:, sl], rt=rt[:, sl], kt=kt[:, sl],
                bt=bt[:, sl], v=v[:, sl], e_ref=e_ref[:, sl], lam_end=lam_end[:, sl],
                lam_c=lam_c[:, sl]))
        yield

    for u in units:
        ar = rows(u["at"], u["rt"]).astype(BF16)
        u["sb"] = _dot_nt(ar, stack(u["bt"], head_mask))
        u["sk"] = _dot_nt(ar, stack(u["kt"], head_mask))
    yield
    for u in units:
        u["a_ab"] = jnp.where(strict, u["sb"][:c_len], 0.0)
        u["a_rb"] = jnp.where(incl, u["sb"][c_len:], 0.0).astype(BF16)
        a_ak = jnp.where(strict, u["sk"][:c_len], 0.0)
        a_rk = jnp.where(incl, u["sk"][c_len:], 0.0)
        u["xv"] = _dot(rows(a_ak, a_rk).astype(BF16), stack(u["v"], head_mask))
        u["pw"] = _dot(u["a_ab"].astype(BF16), stack(u["a_ab"], blk_mask))
        u["t_inv"] = eye + u["a_ab"]
    yield
    for lev in range(1, WKV_LOG2_CHUNK):
        for u in units:
            rhs = stack(u["pw"], blk_mask)
            if lev < WKV_LOG2_CHUNK - 1:
                prod = _dot(rows(u["pw"], u["t_inv"]).astype(BF16), rhs)
                u["pw"] = prod[:c_len]
                u["t_inv"] = u["t_inv"] + prod[c_len:]
            else:
                u["t_inv"] = u["t_inv"] + _dot(u["t_inv"].astype(BF16), rhs)
        yield
    for u in units:
        z16 = z_ref[u["zi"]].astype(BF16)
        ra = rows(u["rt"] * u["e_ref"], u["at"] * u["e_ref"]).astype(BF16)
        zs = _dot_nt(ra, z16)
        u["ys"] = zs[:c_len]
        u["rhs_u"] = zs[c_len:] + u["xv"][:c_len]
    yield
    for u in units:
        u["uu"] = _dot(u["t_inv"].astype(BF16), stack(u["rhs_u"], head_mask))
    yield
    for u in units:
        y = u["ys"] + _dot(u["a_rb"], stack(u["uu"], head_mask)) + u["xv"][c_len:]
        y_ref[u["s"], :, u["sl"]] = y.astype(BF16)
        bh = (u["bt"] * u["lam_end"]).astype(BF16)
        kh = (u["kt"] * u["lam_end"]).astype(BF16)
        g = _dot_tn(rows(u["uu"], u["v"]).astype(BF16), rows(bh, kh))
        z_ref[u["zi"]] = z_ref[u["zi"]] * u["lam_c"] + jnp.where(bd_mask, g, 0.0)


def _wkv_body(*refs):
    for _ in _wkv_stages(*refs):
        pass


def _wkv(batch, seq_len, r, lw, k, v, kkn, b):
    n_chunks = seq_len // WKV_CHUNK
    n_seq = WKV_SEQS if batch % WKV_SEQS == 0 else 1
    view = lambda a: a.reshape(batch, seq_len, RWKV_WIDTH)
    blk = pl.BlockSpec((n_seq, WKV_CHUNK, RWKV_WIDTH), lambda bi, ci: (bi, ci, 0))
    y = pl.pallas_call(
        _wkv_body,
        grid=(batch // n_seq, n_chunks),
        in_specs=[blk] * 6,
        out_specs=blk,
        out_shape=jax.ShapeDtypeStruct((batch, seq_len, RWKV_WIDTH), BF16),
        scratch_shapes=[pltpu.VMEM((n_seq * RWKV_WIDTH // MXU_DIM, MXU_DIM, MXU_DIM), F32)],
        compiler_params=pltpu.CompilerParams(
            dimension_semantics=("arbitrary", "arbitrary"), vmem_limit_bytes=VMEM_LIMIT_BYTES),
        name="wkv",
    )(*(view(a) for a in (r, lw, k, v, kkn, b)))
    return y.reshape(batch * seq_len, RWKV_WIDTH)


def _phase_perm(dil):
    n = MXU_DIM // dil
    out = lax.broadcasted_iota(jnp.int32, (MXU_DIM, MXU_DIM), 0)
    src = lax.broadcasted_iota(jnp.int32, (MXU_DIM, MXU_DIM), 1)
    return jnp.where(src == (out % n) * dil + out // n, 1.0, 0.0).astype(BF16)


def _attn_regroup(seq_len, dil, q_ref, k_ref, v_ref, qsub, ksub, vsub):
    blk = ATTN_BLOCK
    sub = seq_len // dil
    n = MXU_DIM // dil
    ksub[0:blk, :] = jnp.zeros((blk, ATTN_GROUP_WIDTH), BF16)
    vsub[0:blk, :] = jnp.zeros((blk, ATTN_GROUP_WIDTH), BF16)
    perm = None if dil == 1 else _phase_perm(dil)
    for src, dst, off in ((q_ref, qsub, 0), (k_ref, ksub, blk), (v_ref, vsub, blk)):
        if dil == 1:
            dst[off:off + seq_len, :] = src[...]
            continue
        for t in range(seq_len // MXU_DIM):
            y = _dot(perm, src[t * MXU_DIM:(t + 1) * MXU_DIM, :]).astype(BF16)
            for c in range(dil):
                dst[off + c * sub + t * n:off + c * sub + (t + 1) * n, :] = y[c * n:(c + 1) * n]


def _attn_scatter(seq_len, dil, osub, lsub, o_ref, l_ref):
    sub = seq_len // dil
    n = MXU_DIM // dil
    if dil == 1:
        o_ref[...] = osub[...]
        l_ref[...] = lsub[...]
        return
    perm = _phase_perm(dil)
    for t in range(seq_len // MXU_DIM):
        rows = lambda ref: jnp.concatenate(
            [ref[c * sub + t * n:c * sub + (t + 1) * n, :] for c in range(dil)], axis=0)
        out = slice(t * MXU_DIM, (t + 1) * MXU_DIM)
        o_ref[out, :] = _dot_tn(perm, rows(osub)).astype(BF16)
        l = rows(lsub)
        hi = l.astype(BF16)
        rem = l - hi.astype(F32)
        mid = rem.astype(BF16)
        lo = (rem - mid.astype(F32)).astype(BF16)
        l_ref[out, :] = _dot_tn(perm, hi) + _dot_tn(perm, mid) + _dot_tn(perm, lo)


def _attn_body(seq_len, q_ref, k_ref, v_ref, o_ref, l_ref, qsub, ksub, vsub, osub, lsub):
    g = pl.program_id(1)
    blk = ATTN_BLOCK
    n_blocks = seq_len // blk

    for gi, (_, dil) in enumerate(ATTN_PAIRS):
        pl.when(g == gi)(functools.partial(
            _attn_regroup, seq_len, dil, q_ref, k_ref, v_ref, qsub, ksub, vsub))

    blocks_per_sub = jnp.int32(n_blocks)
    for gi, (_, dil) in enumerate(ATTN_PAIRS):
        blocks_per_sub = jnp.where(g == gi, n_blocks // dil, blocks_per_sub)

    qi = lax.broadcasted_iota(jnp.int32, (blk, 2 * blk), 0)
    kj = lax.broadcasted_iota(jnp.int32, (blk, 2 * blk), 1)
    dist = qi + blk - kj
    bias_any = jnp.where((dist >= 0) & (dist <= blk), 0.0, NEG_INF)
    bias_first = jnp.where(kj >= blk, bias_any, NEG_INF)
    lane = lax.broadcasted_iota(jnp.int32, (blk, ATTN_GROUP_WIDTH), 1)
    head_mask = [(lane // HEAD_DIM) == h for h in range(HEADS_PER_TILE)]

    def blocks(it, carry):
        ids = [it * ATTN_UNROLL + j for j in range(ATTN_UNROLL)]
        r0s = [pl.multiple_of(i * blk, blk) for i in ids]
        ss, ps, dens, ms, os_ = [], [], [], [], []
        for i, r0 in zip(ids, r0s):
            q = qsub[pl.ds(r0, blk), :]
            qs = jnp.concatenate([jnp.where(m, q, jnp.zeros_like(q)) for m in head_mask], axis=0)
            bias = jnp.where((i % blocks_per_sub) > 0, bias_any, bias_first)
            s_ = _dot_nt(qs, ksub[pl.ds(r0, 2 * blk), :])
            ss.append((s_.reshape(HEADS_PER_TILE, blk, 2 * blk) + bias[None]).reshape(s_.shape))
        for s_ in ss:
            ms.append(jnp.max(s_, axis=-1, keepdims=True))
        for s_, m in zip(ss, ms):
            ps.append(jnp.exp(s_ - m))
        for p in ps:
            dens.append(jnp.sum(p, axis=-1, keepdims=True))
        for p, r0 in zip(ps, r0s):
            os_.append(_dot(p.astype(BF16), vsub[pl.ds(r0, 2 * blk), :]))
        for o_all, den, m, r0 in zip(os_, dens, ms, r0s):
            o_all = o_all * (1.0 / den)
            lse = m + jnp.log(den)
            o = jnp.zeros((blk, ATTN_GROUP_WIDTH), F32)
            l = jnp.zeros((blk, ATTN_GROUP_WIDTH), F32)
            for h in range(HEADS_PER_TILE):
                rows = slice(h * blk, (h + 1) * blk)
                o = jnp.where(head_mask[h], o_all[rows], o)
                l = jnp.where(head_mask[h], lse[rows], l)
            osub[pl.ds(r0, blk), :] = o.astype(BF16)
            lsub[pl.ds(r0, blk), :] = l
        return carry

    lax.fori_loop(0, n_blocks // ATTN_UNROLL, blocks, 0)

    for gi, (_, dil) in enumerate(ATTN_PAIRS):
        pl.when(g == gi)(functools.partial(_attn_scatter, seq_len, dil, osub, lsub, o_ref, l_ref))


def _attn(batch, seq_len, aq, ak, av):
    view = lambda a: a.reshape(batch, seq_len, ATTN_WIDTH)
    spec = pl.BlockSpec((None, seq_len, ATTN_GROUP_WIDTH), lambda bi, g: (bi, 0, g))
    sub_rows = seq_len + ATTN_BLOCK
    o, l = pl.pallas_call(
        functools.partial(_attn_body, seq_len),
        grid=(batch, ATTN_GROUPS),
        in_specs=[spec] * 3,
        out_specs=[spec] * 2,
        out_shape=[jax.ShapeDtypeStruct((batch, seq_len, ATTN_WIDTH), BF16),
                   jax.ShapeDtypeStruct((batch, seq_len, ATTN_WIDTH), F32)],
        scratch_shapes=[
            pltpu.VMEM((seq_len, ATTN_GROUP_WIDTH), BF16),
            pltpu.VMEM((sub_rows, ATTN_GROUP_WIDTH), BF16),
            pltpu.VMEM((sub_rows, ATTN_GROUP_WIDTH), BF16),
            pltpu.VMEM((seq_len, ATTN_GROUP_WIDTH), BF16),
            pltpu.VMEM((seq_len, ATTN_GROUP_WIDTH), F32),
        ],
        compiler_params=pltpu.CompilerParams(
            dimension_semantics=("parallel", "arbitrary"), vmem_limit_bytes=VMEM_LIMIT_BYTES),
        name="attn",
    )(view(aq), view(ak), view(av))
    return o.reshape(batch * seq_len, ATTN_WIDTH), l.reshape(batch * seq_len, ATTN_WIDTH)


def _merge_body(x_ref, wkv_ref, bonus_ref, g_ref, gate_ref, o_ref, l_ref,
                lnw_ref, lnb_ref, ones_ref, wpr_ref, wpa_ref, wo_ref, out_ref):
    ones_bd = ones_ref[...]
    inv_n = 1.0 / HEAD_DIM
    wkv = wkv_ref[...].astype(F32)
    dev = wkv - _head_sums(wkv, ones_bd) * inv_n
    var = _head_sums(dev * dev, ones_bd) * inv_n
    y = dev * lax.rsqrt(var + GN_EPS) * lnw_ref[...] + lnb_ref[...]
    y_a = ((y + bonus_ref[...].astype(F32)) * g_ref[...].astype(F32)).astype(BF16)
    pa = _dot(y_a, wpr_ref[...])
    gw = ATTN_GROUP_WIDTH
    l0, l1, l2 = l_ref[:, 0:gw], l_ref[:, gw:2 * gw], l_ref[:, 2 * gw:3 * gw]
    mx = jnp.maximum(jnp.maximum(l0, l1), l2)
    e0, e1, e2 = jnp.exp(l0 - mx), jnp.exp(l1 - mx), jnp.exp(l2 - mx)
    inv = 1.0 / (e0 + e1 + e2)
    pb = None
    for gi, e in enumerate((e0, e1, e2)):
        yb = (o_ref[:, gi * gw:(gi + 1) * gw].astype(F32) * (e * inv)).astype(BF16)
        part = _dot(yb, wpa_ref[gi * gw:(gi + 1) * gw, :])
        pb = part if pb is None else pb + part
    gates = gate_ref[...].astype(F32)
    merged = gates[:, :D_MODEL] * pa + gates[:, D_MODEL:] * pb
    out_ref[...] = x_ref[...] + _dot(merged.astype(BF16), wo_ref[...])


def _merge(x, wkv, bonus, g, gates, attn_o, attn_l, ln_w, ln_b, ones_bd, wpr, wpa, wo):
    t = x.shape[0]
    tm = MERGE_TILE
    row = lambda n: pl.BlockSpec((tm, n), lambda i: (i, 0))
    small = (ln_w, ln_b, ones_bd, wpr, wpa, wo)
    return pl.pallas_call(
        _merge_body,
        grid=(t // tm,),
        in_specs=[row(D_MODEL)] * 4 + [row(2 * D_MODEL)] + [row(ATTN_WIDTH)] * 2
                 + [_resident(a.shape) for a in small],
        out_specs=row(D_MODEL),
        out_shape=jax.ShapeDtypeStruct((t, D_MODEL), F32),
        compiler_params=pltpu.CompilerParams(
            dimension_semantics=("parallel",), vmem_limit_bytes=VMEM_LIMIT_BYTES),
        name="merge",
    )(x, wkv, bonus, g, gates, attn_o, attn_l, *small)


def _pad_rows(a, n):
    return jnp.pad(a, ((0, n - a.shape[0]), (0, 0)))


def _layer(x, seq_len, ffn1_norm, ffn1_w_in, ffn1_w_out, mix_norm, w_in, b_gate, rwkv_mu,
           rwkv_w0, rwkv_w2, rwkv_a0, rwkv_a2, rwkv_g2, rwkv_k_k, rwkv_k_a, rwkv_r_k,
           rwkv_ln_w, rwkv_ln_b, attn_q_norm, attn_k_norm, w_proj_rwkv, w_proj_attn,
           w_out, ffn2_norm, ffn2_w_in, ffn2_w_out):
    batch = x.shape[0] // seq_len
    row = lambda a: a.reshape(1, -1)

    w_rw = jnp.pad(w_in[:, :RWKV_COLS].astype(BF16), ((0, 0), (0, RW_PAD - RWKV_COLS)))
    w_ag = w_in[:, RWKV_COLS:].astype(BF16)
    mu_pad = jnp.pad(rwkv_mu, (0, RW_PAD - RWKV_COLS)).reshape(1, RW_PAD)
    w2_pad = _pad_rows(rwkv_w2, LORA_PAD).astype(BF16)
    a2_pad = jnp.pad(rwkv_a2, ((DECAY_LORA, 0), (0, 0))).astype(BF16)
    g2_pad = _pad_rows(rwkv_g2, GATE_LORA_PAD).astype(BF16)
    n_heads = ATTN_WIDTH // HEAD_DIM
    q_gain = row(jnp.tile(attn_q_norm, n_heads) * HEAD_DIM ** -0.5)
    k_gain = row(jnp.tile(attn_k_norm, n_heads))
    idx = jnp.arange(MXU_DIM) // HEAD_DIM
    ones_bd = (idx[:, None] == idx[None, :]).astype(BF16)

    x = _ffn(x, row(ffn1_norm), ffn1_w_in.astype(BF16), ffn1_w_out.astype(BF16))
    (r, lw, k, v, kkn, b, bonus, g, aq, ak, av, gates) = _proj(
        x, seq_len, row(mix_norm), w_rw, w_ag, row(b_gate), mu_pad, row(rwkv_w0), w2_pad,
        row(rwkv_a0), a2_pad, g2_pad, row(rwkv_k_k), row(rwkv_k_a), row(rwkv_r_k),
        q_gain, k_gain, ones_bd)
    wkv = _wkv(batch, seq_len, r, lw, k, v, kkn, b)
    attn_o, attn_l = _attn(batch, seq_len, aq, ak, av)
    x = _merge(x, wkv, bonus, g, gates, attn_o, attn_l, row(rwkv_ln_w), row(rwkv_ln_b), ones_bd,
               w_proj_rwkv.astype(BF16), w_proj_attn.astype(BF16), w_out.astype(BF16))
    return _ffn(x, row(ffn2_norm), ffn2_w_in.astype(BF16), ffn2_w_out.astype(BF16))


def kernel(x, ffn1_norm, ffn1_w_in, ffn1_w_out, mix_norm, w_in, b_gate, rwkv_mu, rwkv_w0, rwkv_w2, rwkv_a0, rwkv_a2, rwkv_g2, rwkv_k_k, rwkv_k_a, rwkv_r_k, rwkv_ln_w, rwkv_ln_b, attn_q_norm, attn_k_norm, w_proj_rwkv, w_proj_attn, w_out, ffn2_norm, ffn2_w_in, ffn2_w_out):
    batch, seq_len, d = x.shape
    params = (ffn1_norm, ffn1_w_in, ffn1_w_out, mix_norm, w_in, b_gate, rwkv_mu, rwkv_w0, rwkv_w2,
              rwkv_a0, rwkv_a2, rwkv_g2, rwkv_k_k, rwkv_k_a, rwkv_r_k, rwkv_ln_w, rwkv_ln_b,
              attn_q_norm, attn_k_norm, w_proj_rwkv, w_proj_attn, w_out, ffn2_norm, ffn2_w_in,
              ffn2_w_out)
    h = x.reshape(batch * seq_len, d)
    for layer in range(ffn1_norm.shape[0]):
        h = _layer(h, seq_len, *(p[layer] for p in params))
    return h.reshape(batch, seq_len, d)
```

```python
import functools

import jax
import jax.numpy as jnp
from jax import lax
from jax.experimental import pallas as pl
from jax.experimental.pallas import tpu as pltpu

F32 = jnp.float32
BF16 = jnp.bfloat16

D_MODEL = 1024
HEAD_DIM = 64
RWKV_WIDTH = 1024
DECAY_LORA = 64
AAA_LORA = 64
GATE_LORA = 160
RWKV_COLS = 3 * RWKV_WIDTH + DECAY_LORA + AAA_LORA + GATE_LORA
GN_EPS = 64e-5
ATTN_PAIRS = ((128, 1), (512, 4), (2048, 16))
ATTN_GROUPS = 3
ATTN_GROUP_WIDTH = 256
ATTN_WIDTH = ATTN_GROUPS * ATTN_GROUP_WIDTH
ATTN_BLOCK = 128
D_FF = 2816
RMS_EPS = 1e-6
NEG_INF = -1e30

LANES = 128
SUBLANES = 8
MXU_DIM = 256
HEADS_PER_TILE = MXU_DIM // HEAD_DIM
VMEM_LIMIT_BYTES = 56 * 1024 * 1024

LORA_PAD = 128
GATE_LORA_PAD = 256
RW_PAD = 3 * RWKV_WIDTH + LORA_PAD + GATE_LORA_PAD
ATT0 = RW_PAD
GATE0 = ATT0 + 3 * ATTN_WIDTH
IN_PAD = GATE0 + 2 * D_MODEL

FF_CHUNK = MXU_DIM
FFN_TILE = 512
PROJ_TILE = 256
MERGE_TILE = 512
WKV_CHUNK = 64
WKV_LOG2_CHUNK = 6
ATTN_UNROLL = 4
WKV_SEQS = 8
WKV_GROUPS = 8
WKV_STAGGER = 1


def _dot(a, b):
    return jnp.dot(a, b, preferred_element_type=F32)


def _dot_nt(a, b):
    return lax.dot_general(a, b, (((1,), (1,)), ((), ())), preferred_element_type=F32)


def _dot_tn(a, b):
    return lax.dot_general(a, b, (((0,), (0,)), ((), ())), preferred_element_type=F32)


def _rms_norm_bf16(x, gain):
    ms = jnp.mean(x * x, axis=-1, keepdims=True)
    return (x * lax.rsqrt(ms + RMS_EPS) * gain).astype(BF16)


def _head_sums(t, ones_bd):
    parts = []
    for q in range(t.shape[-1] // MXU_DIM):
        parts.append(_dot(t[:, q * MXU_DIM:(q + 1) * MXU_DIM].astype(BF16), ones_bd))
    return jnp.concatenate(parts, axis=-1)


def _resident(shape):
    return pl.BlockSpec(shape, lambda *_: (0,) * len(shape), pipeline_mode=pl.Buffered(1))


def _ffn_body(x_ref, gain_ref, w_in_ref, w_out_ref, o_ref):
    x = x_ref[...]
    h = _rms_norm_bf16(x, gain_ref[...])
    acc = None
    for c in range(D_FF // FF_CHUNK):
        lo = c * FF_CHUNK
        gate = _dot(h, w_in_ref[:, lo:lo + FF_CHUNK])
        up = _dot(h, w_in_ref[:, D_FF + lo:D_FF + lo + FF_CHUNK])
        act = (gate * jax.nn.sigmoid(gate) * up).astype(BF16)
        part = _dot(act, w_out_ref[lo:lo + FF_CHUNK, :])
        acc = part if acc is None else acc + part
    o_ref[...] = x + 0.5 * acc


def _ffn(x, gain, w_in, w_out):
    t = x.shape[0]
    tm = FFN_TILE
    return pl.pallas_call(
        _ffn_body,
        grid=(t // tm,),
        in_specs=[
            pl.BlockSpec((tm, D_MODEL), lambda i: (i, 0)),
            _resident((1, D_MODEL)),
            _resident((D_MODEL, 2 * D_FF)),
            _resident((D_FF, D_MODEL)),
        ],
        out_specs=pl.BlockSpec((tm, D_MODEL), lambda i: (i, 0)),
        out_shape=jax.ShapeDtypeStruct((t, D_MODEL), F32),
        compiler_params=pltpu.CompilerParams(
            dimension_semantics=("parallel",), vmem_limit_bytes=VMEM_LIMIT_BYTES),
        name="ffn",
    )(x, gain, w_in, w_out)


def _softplus(z):
    return jnp.maximum(z, 0.0) + jnp.log(1.0 + jnp.exp(-jnp.abs(z)))


def _proj_body(tiles_per_seq,
               x_ref, gain_ref, w_ref, wag_ref, bg_ref, mu_ref, w0_ref, w2_ref, a0_ref, a2_ref, g2_ref,
               kk_ref, ka_ref, rk_ref, qg_ref, kg_ref, ones_ref,
               r_o, lw_o, k_o, v_o, kkn_o, b_o, bonus_o, g_o, aq_o, ak_o, av_o, gate_o,
               carry_ref):
    i = pl.program_id(0)
    tm = x_ref.shape[0]
    w = RWKV_WIDTH
    sw = MXU_DIM
    h = _rms_norm_bf16(x_ref[...], gain_ref[...])
    ones_bd = ones_ref[...]
    row8 = lax.broadcasted_iota(jnp.int32, (SUBLANES, 1), 0)
    inv_n = 1.0 / HEAD_DIM

    @pl.when((i % tiles_per_seq) == 0)
    def _():
        carry_ref[...] = jnp.zeros_like(carry_ref)

    def mm(lo, hi):
        if hi <= ATT0:
            return _dot(h, w_ref[:, lo:hi])
        return _dot(h, wag_ref[:, lo - ATT0:hi - ATT0])

    def shift(p, lo, hi):
        rolled = pltpu.roll(p, 1, 0)
        head = jnp.where(row8 == 0, carry_ref[:, lo:hi], rolled[:SUBLANES])
        prev = jnp.concatenate([head, rolled[SUBLANES:]], axis=0)
        carry_ref[:, lo:hi] = p[tm - 1:tm, :]
        return p + (prev - p) * mu_ref[:, lo:hi]

    def head_sum(t):
        return _dot(t.astype(BF16), ones_bd)

    def rwkv_mm(q):
        c = q * sw
        return [mm(j * w + c, j * w + c + sw) for j in range(3)]

    def rwkv_strip(q, prk, lora16, tanh16, sg16):
        c = q * sw
        cols = slice(c, c + sw)
        r = shift(prk[0], c, c + sw)
        k0 = shift(prk[1], w + c, w + c + sw)
        v = shift(prk[2], 2 * w + c, 2 * w + c + sw)
        wraw = w0_ref[:, cols] + _dot(tanh16, w2_ref[:, cols])
        lw_o[:, cols] = -jnp.exp(-_softplus(-wraw) - 0.5)
        a = jax.nn.sigmoid(a0_ref[:, cols] + _dot(lora16, a2_ref[:, cols]))
        g_o[:, cols] = _dot(sg16, g2_ref[:, cols]).astype(BF16)
        kk = k0 * kk_ref[:, cols]
        kkn = kk * lax.rsqrt(jnp.maximum(head_sum(kk * kk), 1e-24))
        k = k0 * (1.0 + (a - 1.0) * ka_ref[:, cols])
        r_o[:, cols] = r.astype(BF16)
        k_o[:, cols] = k.astype(BF16)
        v_o[:, cols] = v.astype(BF16)
        kkn_o[:, cols] = kkn.astype(BF16)
        b_o[:, cols] = (kkn * a).astype(BF16)
        bonus_o[:, cols] = (head_sum(r * k * rk_ref[:, cols]) * v).astype(BF16)

    def attn_mm(g):
        c = ATT0 + g * sw
        return [mm(c + j * ATTN_WIDTH, c + j * ATTN_WIDTH + sw) for j in range(3)]

    def attn_strip(g, pqkv):
        cols = slice(g * sw, (g + 1) * sw)
        pq, pk, pv = pqkv
        qn = pq * lax.rsqrt(head_sum(pq * pq) * inv_n + RMS_EPS) * qg_ref[:, cols]
        kn = pk * lax.rsqrt(head_sum(pk * pk) * inv_n + RMS_EPS) * kg_ref[:, cols]
        aq_o[:, cols] = qn.astype(BF16)
        ak_o[:, cols] = kn.astype(BF16)
        av_o[:, cols] = pv.astype(BF16)

    def gate_mm(q):
        return mm(GATE0 + q * 2 * sw, GATE0 + (q + 1) * 2 * sw)

    def gate_strip(q, pg):
        cols = slice(q * 2 * sw, (q + 1) * 2 * sw)
        gate_o[:, cols] = jax.nn.sigmoid(pg + bg_ref[:, cols]).astype(BF16)

    p_l = mm(3 * w, RW_PAD)
    nxt = rwkv_mm(0)
    ps_l = shift(p_l, 3 * w, RW_PAD)
    lora = ps_l[:, :LORA_PAD]
    lora16 = lora.astype(BF16)
    tanh16 = jnp.tanh(lora).astype(BF16)
    sg16 = jax.nn.sigmoid(ps_l[:, LORA_PAD:]).astype(BF16)

    n_rw = w // sw
    n_gate = 2 * D_MODEL // (2 * sw)
    for q in range(n_rw):
        cur = nxt
        nxt = rwkv_mm(q + 1) if q + 1 < n_rw else attn_mm(0)
        rwkv_strip(q, cur, lora16, tanh16, sg16)
    for g in range(ATTN_GROUPS):
        cur = nxt
        nxt = attn_mm(g + 1) if g + 1 < ATTN_GROUPS else gate_mm(0)
        attn_strip(g, cur)
    for q in range(n_gate):
        cur = nxt
        nxt = gate_mm(q + 1) if q + 1 < n_gate else None
        gate_strip(q, cur)


def _proj(x, seq_len, gain, w, wag, bg, mu, w0, w2, a0, a2, g2, k_k, k_a, r_k, qg, kg, ones_bd):
    t = x.shape[0]
    tm = PROJ_TILE
    row = lambda n: pl.BlockSpec((tm, n), lambda i: (i, 0))
    wide = jax.ShapeDtypeStruct((t, RWKV_WIDTH), BF16)
    wide32 = jax.ShapeDtypeStruct((t, RWKV_WIDTH), F32)
    att = jax.ShapeDtypeStruct((t, ATTN_WIDTH), BF16)
    return pl.pallas_call(
        functools.partial(_proj_body, seq_len // tm),
        grid=(t // tm,),
        in_specs=[row(D_MODEL)] + [_resident(a.shape) for a in
                                   (gain, w, wag, bg, mu, w0, w2, a0, a2, g2, k_k, k_a, r_k, qg, kg, ones_bd)],
        out_specs=[row(RWKV_WIDTH)] * 8 + [row(ATTN_WIDTH)] * 3 + [row(2 * D_MODEL)],
        out_shape=[wide, wide32] + [wide] * 6 + [att] * 3
                  + [jax.ShapeDtypeStruct((t, 2 * D_MODEL), BF16)],
        scratch_shapes=[pltpu.VMEM((1, RW_PAD), F32)],
        compiler_params=pltpu.CompilerParams(
            dimension_semantics=("arbitrary",), vmem_limit_bytes=VMEM_LIMIT_BYTES),
        name="proj",
    )(x, gain, w, wag, bg, mu, w0, w2, a0, a2, g2, k_k, k_a, r_k, qg, kg, ones_bd)


def _wkv_stages(seqs, r_ref, lw_ref, k_ref, v_ref, kkn_ref, b_ref, y_ref, z_ref):
    c_len = WKV_CHUNK
    n_tiles = RWKV_WIDTH // MXU_DIM

    trow = lax.broadcasted_iota(jnp.int32, (c_len, 1), 0)
    lane = lax.broadcasted_iota(jnp.int32, (c_len, MXU_DIM), 1)
    rowc = lax.broadcasted_iota(jnp.int32, (c_len, MXU_DIM), 0)
    head_mask = [(lane // HEAD_DIM) == h for h in range(HEADS_PER_TILE)]
    blk_mask = [(lane // c_len) == h for h in range(HEADS_PER_TILE)]
    col_in_blk = lane % c_len
    strict = col_in_blk < rowc
    incl = col_in_blk <= rowc
    eye = jnp.where(col_in_blk == rowc, 1.0, 0.0)
    br = lax.broadcasted_iota(jnp.int32, (MXU_DIM, MXU_DIM), 0) // HEAD_DIM
    bc = lax.broadcasted_iota(jnp.int32, (MXU_DIM, MXU_DIM), 1) // HEAD_DIM
    bd_mask = br == bc

    def stack(x, masks):
        return jnp.concatenate([jnp.where(m, x, 0.0) for m in masks], axis=0).astype(BF16)

    def rows(*xs):
        return jnp.concatenate(xs, axis=0)

    units = []
    for s in seqs:
        lw = lw_ref[s]
        cum = lw
        for sh in (1, 2, 4, 8, 16, 32):
            cum = cum + jnp.where(trow >= sh, pltpu.roll(cum, sh, 0), 0.0)
        ref = cum[c_len // 2 - 1:c_len // 2, :]
        e_fwd = jnp.exp(cum - ref)
        e_prev = jnp.exp(cum - lw - ref)
        e_bwd = jnp.exp(ref - cum)
        e_ref = jnp.exp(ref)
        lam_end = e_fwd[c_len - 1:c_len, :]
        lam_c = jnp.exp(cum[c_len - 1:c_len, :])
        at = -kkn_ref[s].astype(F32) * e_prev
        rt = r_ref[s].astype(F32) * e_fwd
        kt = k_ref[s].astype(F32) * e_bwd
        bt = b_ref[s].astype(F32) * e_bwd
        v = v_ref[s].astype(F32)
        for q in range(n_tiles):
            sl = slice(q * MXU_DIM, (q + 1) * MXU_DIM)
            units.append(dict(
                s=s, sl=sl, zi=s * n_tiles + q, at=at[:, sl], rt=rt[:, sl], kt=kt[:, sl],
                bt=bt[:, sl], v=v[:, sl], e_ref=e_ref[:, sl], lam_end=lam_end[:, sl],
                lam_c=lam_c[:, sl]))
        yield

    for u in units:
        ar = rows(u["at"], u["rt"]).astype(BF16)
        u["sb"] = _dot_nt(ar, stack(u["bt"], head_mask))
        u["sk"] = _dot_nt(ar, stack(u["kt"], head_mask))
    yield
    for u in units:
        u["a_ab"] = jnp.where(strict, u["sb"][:c_len], 0.0)
        u["a_rb"] = jnp.where(incl, u["sb"][c_len:], 0.0).astype(BF16)
        a_ak = jnp.where(strict, u["sk"][:c_len], 0.0)
        a_rk = jnp.where(incl, u["sk"][c_len:], 0.0)
        u["xv"] = _dot(rows(a_ak, a_rk).astype(BF16), stack(u["v"], head_mask))
        u["pw"] = _dot(u["a_ab"].astype(BF16), stack(u["a_ab"], blk_mask))
        u["t_inv"] = eye + u["a_ab"]
    yield
    for lev in range(1, WKV_LOG2_CHUNK):
        for u in units:
            rhs = stack(u["pw"], blk_mask)
            if lev < WKV_LOG2_CHUNK - 1:
                prod = _dot(rows(u["pw"], u["t_inv"]).astype(BF16), rhs)
                u["pw"] = prod[:c_len]
                u["t_inv"] = u["t_inv"] + prod[c_len:]
            else:
                u["t_inv"] = u["t_inv"] + _dot(u["t_inv"].astype(BF16), rhs)
        yield
    for u in units:
        z16 = z_ref[u["zi"]].astype(BF16)
        ra = rows(u["rt"] * u["e_ref"], u["at"] * u["e_ref"]).astype(BF16)
        zs = _dot_nt(ra, z16)
        u["ys"] = zs[:c_len]
        u["rhs_u"] = zs[c_len:] + u["xv"][:c_len]
    yield
    for u in units:
        u["uu"] = _dot(u["t_inv"].astype(BF16), stack(u["rhs_u"], head_mask))
    yield
    for u in units:
        y = u["ys"] + _dot(u["a_rb"], stack(u["uu"], head_mask)) + u["xv"][c_len:]
        y_ref[u["s"], :, u["sl"]] = y.astype(BF16)
        bh = (u["bt"] * u["lam_end"]).astype(BF16)
        kh = (u["kt"] * u["lam_end"]).astype(BF16)
        g = _dot_tn(rows(u["uu"], u["v"]).astype(BF16), rows(bh, kh))
        z_ref[u["zi"]] = z_ref[u["zi"]] * u["lam_c"] + jnp.where(bd_mask, g, 0.0)


def _wkv_body(*refs):
    z_ref = refs[-1]
    n_seq = refs[0].shape[0]

    @pl.when(pl.program_id(1) == 0)
    def _():
        z_ref[...] = jnp.zeros_like(z_ref)

    n_groups = min(WKV_GROUPS, n_seq)
    per = n_seq // n_groups
    gens = [_wkv_stages(range(g * per, (g + 1) * per), *refs) for g in range(n_groups)]
    live = [True] * n_groups
    tick = 0
    while any(live):
        for g, gen in enumerate(gens):
            if live[g] and tick >= g * WKV_STAGGER:
                live[g] = next(gen, "done") != "done"
        tick += 1


def _wkv(batch, seq_len, r, lw, k, v, kkn, b):
    n_chunks = seq_len // WKV_CHUNK
    n_seq = WKV_SEQS if batch % WKV_SEQS == 0 else 1
    view = lambda a: a.reshape(batch, seq_len, RWKV_WIDTH)
    blk = pl.BlockSpec((n_seq, WKV_CHUNK, RWKV_WIDTH), lambda bi, ci: (bi, ci, 0))
    y = pl.pallas_call(
        _wkv_body,
        grid=(batch // n_seq, n_chunks),
        in_specs=[blk] * 6,
        out_specs=blk,
        out_shape=jax.ShapeDtypeStruct((batch, seq_len, RWKV_WIDTH), BF16),
        scratch_shapes=[pltpu.VMEM((n_seq * RWKV_WIDTH // MXU_DIM, MXU_DIM, MXU_DIM), F32)],
        compiler_params=pltpu.CompilerParams(
            dimension_semantics=("arbitrary", "arbitrary"), vmem_limit_bytes=VMEM_LIMIT_BYTES),
        name="wkv",
    )(*(view(a) for a in (r, lw, k, v, kkn, b)))
    return y.reshape(batch * seq_len, RWKV_WIDTH)


def _phase_perm(dil):
    n = MXU_DIM // dil
    out = lax.broadcasted_iota(jnp.int32, (MXU_DIM, MXU_DIM), 0)
    src = lax.broadcasted_iota(jnp.int32, (MXU_DIM, MXU_DIM), 1)
    return jnp.where(src == (out % n) * dil + out // n, 1.0, 0.0).astype(BF16)


def _attn_regroup(seq_len, dil, q_ref, k_ref, v_ref, qsub, ksub, vsub):
    blk = ATTN_BLOCK
    sub = seq_len // dil
    n = MXU_DIM // dil
    ksub[0:blk, :] = jnp.zeros((blk, ATTN_GROUP_WIDTH), BF16)
    vsub[0:blk, :] = jnp.zeros((blk, ATTN_GROUP_WIDTH), BF16)
    perm = None if dil == 1 else _phase_perm(dil)
    for src, dst, off in ((q_ref, qsub, 0), (k_ref, ksub, blk), (v_ref, vsub, blk)):
        if dil == 1:
            dst[off:off + seq_len, :] = src[...]
            continue
        for t in range(seq_len // MXU_DIM):
            y = _dot(perm, src[t * MXU_DIM:(t + 1) * MXU_DIM, :]).astype(BF16)
            for c in range(dil):
                dst[off + c * sub + t * n:off + c * sub + (t + 1) * n, :] = y[c * n:(c + 1) * n]


def _attn_scatter(seq_len, dil, osub, lsub, o_ref, l_ref):
    sub = seq_len // dil
    n = MXU_DIM // dil
    if dil == 1:
        o_ref[...] = osub[...]
        l_ref[...] = lsub[...]
        return
    perm = _phase_perm(dil)
    for t in range(seq_len // MXU_DIM):
        rows = lambda ref: jnp.concatenate(
            [ref[c * sub + t * n:c * sub + (t + 1) * n, :] for c in range(dil)], axis=0)
        out = slice(t * MXU_DIM, (t + 1) * MXU_DIM)
        o_ref[out, :] = _dot_tn(perm, rows(osub)).astype(BF16)
        l = rows(lsub)
        hi = l.astype(BF16)
        rem = l - hi.astype(F32)
        mid = rem.astype(BF16)
        lo = (rem - mid.astype(F32)).astype(BF16)
        l_ref[out, :] = _dot_tn(perm, hi) + _dot_tn(perm, mid) + _dot_tn(perm, lo)


def _attn_body(seq_len, q_ref, k_ref, v_ref, o_ref, l_ref, qsub, ksub, vsub, osub, lsub):
    g = pl.program_id(1)
    blk = ATTN_BLOCK
    n_blocks = seq_len // blk

    for gi, (_, dil) in enumerate(ATTN_PAIRS):
        pl.when(g == gi)(functools.partial(
            _attn_regroup, seq_len, dil, q_ref, k_ref, v_ref, qsub, ksub, vsub))

    blocks_per_sub = jnp.int32(n_blocks)
    for gi, (_, dil) in enumerate(ATTN_PAIRS):
        blocks_per_sub = jnp.where(g == gi, n_blocks // dil, blocks_per_sub)

    qi = lax.broadcasted_iota(jnp.int32, (blk, 2 * blk), 0)
    kj = lax.broadcasted_iota(jnp.int32, (blk, 2 * blk), 1)
    dist = qi + blk - kj
    bias_any = jnp.where((dist >= 0) & (dist <= blk), 0.0, NEG_INF)
    bias_first = jnp.where(kj >= blk, bias_any, NEG_INF)
    lane = lax.broadcasted_iota(jnp.int32, (blk, ATTN_GROUP_WIDTH), 1)
    head_mask = [(lane // HEAD_DIM) == h for h in range(HEADS_PER_TILE)]

    def blocks(it, carry):
        ids = [it * ATTN_UNROLL + j for j in range(ATTN_UNROLL)]
        r0s = [pl.multiple_of(i * blk, blk) for i in ids]
        ss, ps, dens, ms, os_ = [], [], [], [], []
        for i, r0 in zip(ids, r0s):
            q = qsub[pl.ds(r0, blk), :]
            qs = jnp.concatenate([jnp.where(m, q, jnp.zeros_like(q)) for m in head_mask], axis=0)
            bias = jnp.where((i % blocks_per_sub) > 0, bias_any, bias_first)
            s_ = _dot_nt(qs, ksub[pl.ds(r0, 2 * blk), :])
            ss.append((s_.reshape(HEADS_PER_TILE, blk, 2 * blk) + bias[None]).reshape(s_.shape))
        for s_ in ss:
            ms.append(jnp.max(s_, axis=-1, keepdims=True))
        for s_, m in zip(ss, ms):
            ps.append(jnp.exp(s_ - m))
        for p in ps:
            dens.append(jnp.sum(p, axis=-1, keepdims=True))
        for p, r0 in zip(ps, r0s):
            os_.append(_dot(p.astype(BF16), vsub[pl.ds(r0, 2 * blk), :]))
        for o_all, den, m, r0 in zip(os_, dens, ms, r0s):
            o_all = o_all * (1.0 / den)
            lse = m + jnp.log(den)
            o = jnp.zeros((blk, ATTN_GROUP_WIDTH), F32)
            l = jnp.zeros((blk, ATTN_GROUP_WIDTH), F32)
            for h in range(HEADS_PER_TILE):
                rows = slice(h * blk, (h + 1) * blk)
                o = jnp.where(head_mask[h], o_all[rows], o)
                l = jnp.where(head_mask[h], lse[rows], l)
            osub[pl.ds(r0, blk), :] = o.astype(BF16)
            lsub[pl.ds(r0, blk), :] = l
        return carry

    lax.fori_loop(0, n_blocks // ATTN_UNROLL, blocks, 0)

    for gi, (_, dil) in enumerate(ATTN_PAIRS):
        pl.when(g == gi)(functools.partial(_attn_scatter, seq_len, dil, osub, lsub, o_ref, l_ref))


def _attn(batch, seq_len, aq, ak, av):
    view = lambda a: a.reshape(batch, seq_len, ATTN_WIDTH)
    spec = pl.BlockSpec((None, seq_len, ATTN_GROUP_WIDTH), lambda bi, g: (bi, 0, g))
    sub_rows = seq_len + ATTN_BLOCK
    o, l = pl.pallas_call(
        functools.partial(_attn_body, seq_len),
        grid=(batch, ATTN_GROUPS),
        in_specs=[spec] * 3,
        out_specs=[spec] * 2,
        out_shape=[jax.ShapeDtypeStruct((batch, seq_len, ATTN_WIDTH), BF16),
                   jax.ShapeDtypeStruct((batch, seq_len, ATTN_WIDTH), F32)],
        scratch_shapes=[
            pltpu.VMEM((seq_len, ATTN_GROUP_WIDTH), BF16),
            pltpu.VMEM((sub_rows, ATTN_GROUP_WIDTH), BF16),
            pltpu.VMEM((sub_rows, ATTN_GROUP_WIDTH), BF16),
            pltpu.VMEM((seq_len, ATTN_GROUP_WIDTH), BF16),
            pltpu.VMEM((seq_len, ATTN_GROUP_WIDTH), F32),
        ],
        compiler_params=pltpu.CompilerParams(
            dimension_semantics=("parallel", "arbitrary"), vmem_limit_bytes=VMEM_LIMIT_BYTES),
        name="attn",
    )(view(aq), view(ak), view(av))
    return o.reshape(batch * seq_len, ATTN_WIDTH), l.reshape(batch * seq_len, ATTN_WIDTH)


def _merge_body(x_ref, wkv_ref, bonus_ref, g_ref, gate_ref, o_ref, l_ref,
                lnw_ref, lnb_ref, ones_ref, wpr_ref, wpa_ref, wo_ref, out_ref):
    ones_bd = ones_ref[...]
    inv_n = 1.0 / HEAD_DIM
    wkv = wkv_ref[...].astype(F32)
    dev = wkv - _head_sums(wkv, ones_bd) * inv_n
    var = _head_sums(dev * dev, ones_bd) * inv_n
    y = dev * lax.rsqrt(var + GN_EPS) * lnw_ref[...] + lnb_ref[...]
    y_a = ((y + bonus_ref[...].astype(F32)) * g_ref[...].astype(F32)).astype(BF16)
    pa = _dot(y_a, wpr_ref[...])
    gw = ATTN_GROUP_WIDTH
    l0, l1, l2 = l_ref[:, 0:gw], l_ref[:, gw:2 * gw], l_ref[:, 2 * gw:3 * gw]
    mx = jnp.maximum(jnp.maximum(l0, l1), l2)
    e0, e1, e2 = jnp.exp(l0 - mx), jnp.exp(l1 - mx), jnp.exp(l2 - mx)
    inv = 1.0 / (e0 + e1 + e2)
    pb = None
    for gi, e in enumerate((e0, e1, e2)):
        yb = (o_ref[:, gi * gw:(gi + 1) * gw].astype(F32) * (e * inv)).astype(BF16)
        part = _dot(yb, wpa_ref[gi * gw:(gi + 1) * gw, :])
        pb = part if pb is None else pb + part
    gates = gate_ref[...].astype(F32)
    merged = gates[:, :D_MODEL] * pa + gates[:, D_MODEL:] * pb
    out_ref[...] = x_ref[...] + _dot(merged.astype(BF16), wo_ref[...])


def _merge(x, wkv, bonus, g, gates, attn_o, attn_l, ln_w, ln_b, ones_bd, wpr, wpa, wo):
    t = x.shape[0]
    tm = MERGE_TILE
    row = lambda n: pl.BlockSpec((tm, n), lambda i: (i, 0))
    small = (ln_w, ln_b, ones_bd, wpr, wpa, wo)
    return pl.pallas_call(
        _merge_body,
        grid=(t // tm,),
        in_specs=[row(D_MODEL)] * 4 + [row(2 * D_MODEL)] + [row(ATTN_WIDTH)] * 2
                 + [_resident(a.shape) for a in small],
        out_specs=row(D_MODEL),
        out_shape=jax.ShapeDtypeStruct((t, D_MODEL), F32),
        compiler_params=pltpu.CompilerParams(
            dimension_semantics=("parallel",), vmem_limit_bytes=VMEM_LIMIT_BYTES),
        name="merge",
    )(x, wkv, bonus, g, gates, attn_o, attn_l, *small)


def _pad_rows(a, n):
    return jnp.pad(a, ((0, n - a.shape[0]), (0, 0)))


def _layer(x, seq_len, ffn1_norm, ffn1_w_in, ffn1_w_out, mix_norm, w_in, b_gate, rwkv_mu,
           rwkv_w0, rwkv_w2, rwkv_a0, rwkv_a2, rwkv_g2, rwkv_k_k, rwkv_k_a, rwkv_r_k,
           rwkv_ln_w, rwkv_ln_b, attn_q_norm, attn_k_norm, w_proj_rwkv, w_proj_attn,
           w_out, ffn2_norm, ffn2_w_in, ffn2_w_out):
    batch = x.shape[0] // seq_len
    row = lambda a: a.reshape(1, -1)

    w_rw = jnp.pad(w_in[:, :RWKV_COLS].astype(BF16), ((0, 0), (0, RW_PAD - RWKV_COLS)))
    w_ag = w_in[:, RWKV_COLS:].astype(BF16)
    mu_pad = jnp.pad(rwkv_mu, (0, RW_PAD - RWKV_COLS)).reshape(1, RW_PAD)
    w2_pad = _pad_rows(rwkv_w2, LORA_PAD).astype(BF16)
    a2_pad = jnp.pad(rwkv_a2, ((DECAY_LORA, 0), (0, 0))).astype(BF16)
    g2_pad = _pad_rows(rwkv_g2, GATE_LORA_PAD).astype(BF16)
    n_heads = ATTN_WIDTH // HEAD_DIM
    q_gain = row(jnp.tile(attn_q_norm, n_heads) * HEAD_DIM ** -0.5)
    k_gain = row(jnp.tile(attn_k_norm, n_heads))
    idx = jnp.arange(MXU_DIM) // HEAD_DIM
    ones_bd = (idx[:, None] == idx[None, :]).astype(BF16)

    x = _ffn(x, row(ffn1_norm), ffn1_w_in.astype(BF16), ffn1_w_out.astype(BF16))
    (r, lw, k, v, kkn, b, bonus, g, aq, ak, av, gates) = _proj(
        x, seq_len, row(mix_norm), w_rw, w_ag, row(b_gate), mu_pad, row(rwkv_w0), w2_pad,
        row(rwkv_a0), a2_pad, g2_pad, row(rwkv_k_k), row(rwkv_k_a), row(rwkv_r_k),
        q_gain, k_gain, ones_bd)
    wkv = _wkv(batch, seq_len, r, lw, k, v, kkn, b)
    attn_o, attn_l = _attn(batch, seq_len, aq, ak, av)
    x = _merge(x, wkv, bonus, g, gates, attn_o, attn_l, row(rwkv_ln_w), row(rwkv_ln_b), ones_bd,
               w_proj_rwkv.astype(BF16), w_proj_attn.astype(BF16), w_out.astype(BF16))
    return _ffn(x, row(ffn2_norm), ffn2_w_in.astype(BF16), ffn2_w_out.astype(BF16))


def kernel(x, ffn1_norm, ffn1_w_in, ffn1_w_out, mix_norm, w_in, b_gate, rwkv_mu, rwkv_w0, rwkv_w2, rwkv_a0, rwkv_a2, rwkv_g2, rwkv_k_k, rwkv_k_a, rwkv_r_k, rwkv_ln_w, rwkv_ln_b, attn_q_norm, attn_k_norm, w_proj_rwkv, w_proj_attn, w_out, ffn2_norm, ffn2_w_in, ffn2_w_out):
    batch, seq_len, d = x.shape
    params = (ffn1_norm, ffn1_w_in, ffn1_w_out, mix_norm, w_in, b_gate, rwkv_mu, rwkv_w0, rwkv_w2,
              rwkv_a0, rwkv_a2, rwkv_g2, rwkv_k_k, rwkv_k_a, rwkv_r_k, rwkv_ln_w, rwkv_ln_b,
              attn_q_norm, attn_k_norm, w_proj_rwkv, w_proj_attn, w_out, ffn2_norm, ffn2_w_in,
              ffn2_w_out)
    h = x.reshape(batch * seq_len, d)
    for layer in range(ffn1_norm.shape[0]):
        h = _layer(h, seq_len, *(p[layer] for p in params))
    return h.reshape(batch, seq_len, d)
```

```python
import functools

import jax
import jax.numpy as jnp
from jax import lax
from jax.experimental import pallas as pl
from jax.experimental.pallas import tpu as pltpu

F32 = jnp.float32
BF16 = jnp.bfloat16

D_MODEL = 1024
HEAD_DIM = 64
RWKV_WIDTH = 1024
DECAY_LORA = 64
AAA_LORA = 64
GATE_LORA = 160
RWKV_COLS = 3 * RWKV_WIDTH + DECAY_LORA + AAA_LORA + GATE_LORA
GN_EPS = 64e-5
ATTN_PAIRS = ((128, 1), (512, 4), (2048, 16))
ATTN_GROUPS = 3
ATTN_GROUP_WIDTH = 256
ATTN_WIDTH = ATTN_GROUPS * ATTN_GROUP_WIDTH
ATTN_BLOCK = 128
D_FF = 2816
RMS_EPS = 1e-6
NEG_INF = -1e30

LANES = 128
SUBLANES = 8
MXU_DIM = 256
HEADS_PER_TILE = MXU_DIM // HEAD_DIM
VMEM_LIMIT_BYTES = 56 * 1024 * 1024

LORA_PAD = 128
GATE_LORA_PAD = 256
RW_PAD = 3 * RWKV_WIDTH + LORA_PAD + GATE_LORA_PAD
ATT0 = RW_PAD
GATE0 = ATT0 + 3 * ATTN_WIDTH

FF_CHUNK = MXU_DIM
FFN_TILE = 512
PROJ_TILE = 256
MERGE_TILE = 512
WKV_CHUNK = 64
WKV_LOG2_CHUNK = 6
ATTN_UNROLL = 4
WKV_SEQS = 8
WKV_GROUPS = 8
WKV_STAGGER = 1


def _dot(a, b):
    return jnp.dot(a, b, preferred_element_type=F32)


def _dot_nt(a, b):
    return lax.dot_general(a, b, (((1,), (1,)), ((), ())), preferred_element_type=F32)


def _dot_tn(a, b):
    return lax.dot_general(a, b, (((0,), (0,)), ((), ())), preferred_element_type=F32)


def _rms_norm_bf16(x, gain):
    ms = jnp.mean(x * x, axis=-1, keepdims=True)
    return (x * lax.rsqrt(ms + RMS_EPS) * gain).astype(BF16)


def _head_sums(t, ones_bd):
    parts = []
    for q in range(t.shape[-1] // MXU_DIM):
        parts.append(_dot(t[:, q * MXU_DIM:(q + 1) * MXU_DIM].astype(BF16), ones_bd))
    return jnp.concatenate(parts, axis=-1)


def _head_sums_xlane(t):
    low_half = lax.broadcasted_iota(jnp.int32, (1, LANES), 1) < HEAD_DIM
    parts = []
    for j in range(t.shape[-1] // LANES):
        ts = t[:, j * LANES:(j + 1) * LANES]
        s0 = jnp.sum(jnp.where(low_half, ts, 0.0), axis=-1, keepdims=True)
        s1 = jnp.sum(jnp.where(low_half, 0.0, ts), axis=-1, keepdims=True)
        parts.append(jnp.where(low_half, s0, s1))
    return jnp.concatenate(parts, axis=-1)


def _resident(shape):
    return pl.BlockSpec(shape, lambda *_: (0,) * len(shape), pipeline_mode=pl.Buffered(1))


def _ffn_body(x_ref, gain_ref, w_in_ref, w_out_ref, o_ref):
    x = x_ref[...]
    h = _rms_norm_bf16(x, gain_ref[...])
    acc = None
    for c in range(D_FF // FF_CHUNK):
        lo = c * FF_CHUNK
        gate = _dot(h, w_in_ref[:, lo:lo + FF_CHUNK])
        up = _dot(h, w_in_ref[:, D_FF + lo:D_FF + lo + FF_CHUNK])
        act = (gate * jax.nn.sigmoid(gate) * up).astype(BF16)
        part = _dot(act, w_out_ref[lo:lo + FF_CHUNK, :])
        acc = part if acc is None else acc + part
    o_ref[...] = x + 0.5 * acc


def _ffn(x, gain, w_in, w_out):
    t = x.shape[0]
    tm = FFN_TILE
    return pl.pallas_call(
        _ffn_body,
        grid=(t // tm,),
        in_specs=[
            pl.BlockSpec((tm, D_MODEL), lambda i: (i, 0)),
            _resident((1, D_MODEL)),
            _resident((D_MODEL, 2 * D_FF)),
            _resident((D_FF, D_MODEL)),
        ],
        out_specs=pl.BlockSpec((tm, D_MODEL), lambda i: (i, 0)),
        out_shape=jax.ShapeDtypeStruct((t, D_MODEL), F32),
        compiler_params=pltpu.CompilerParams(
            dimension_semantics=("parallel",), vmem_limit_bytes=VMEM_LIMIT_BYTES),
        name="ffn",
    )(x, gain, w_in, w_out)


def _softplus(z):
    return jnp.maximum(z, 0.0) + jnp.log(1.0 + jnp.exp(-jnp.abs(z)))


def _proj_body(tiles_per_seq,
               x_ref, gain_ref, w_ref, wag_ref, bg_ref, mu_ref, w0_ref, w2_ref, a0_ref, a2_ref, g2_ref,
               kk_ref, ka_ref, rk_ref, qg_ref, kg_ref,
               r_o, lw_o, k_o, v_o, kkn_o, b_o, bonus_o, g_o, aq_o, ak_o, av_o, gate_o,
               carry_ref):
    i = pl.program_id(0)
    tm = x_ref.shape[0]
    w = RWKV_WIDTH
    sw = MXU_DIM
    h = _rms_norm_bf16(x_ref[...], gain_ref[...])
    row8 = lax.broadcasted_iota(jnp.int32, (SUBLANES, 1), 0)
    inv_n = 1.0 / HEAD_DIM

    @pl.when((i % tiles_per_seq) == 0)
    def _():
        carry_ref[...] = jnp.zeros_like(carry_ref)

    def mm(lo, hi):
        if hi <= ATT0:
            return _dot(h, w_ref[:, lo:hi])
        return _dot(h, wag_ref[:, lo - ATT0:hi - ATT0])

    def shift(p, lo, hi):
        rolled = pltpu.roll(p, 1, 0)
        head = jnp.where(row8 == 0, carry_ref[:, lo:hi], rolled[:SUBLANES])
        prev = jnp.concatenate([head, rolled[SUBLANES:]], axis=0)
        carry_ref[:, lo:hi] = p[tm - 1:tm, :]
        return p + (prev - p) * mu_ref[:, lo:hi]

    head_sum = _head_sums_xlane

    def rwkv_mm(q):
        c = q * sw
        return [mm(j * w + c, j * w + c + sw) for j in range(3)]

    def rwkv_strip(q, prk, lora16, tanh16, sg16):
        c = q * sw
        cols = slice(c, c + sw)
        r = shift(prk[0], c, c + sw)
        k0 = shift(prk[1], w + c, w + c + sw)
        v = shift(prk[2], 2 * w + c, 2 * w + c + sw)
        wraw = w0_ref[:, cols] + _dot(tanh16, w2_ref[:, cols])
        lw_o[:, cols] = -jnp.exp(-_softplus(-wraw) - 0.5)
        a = jax.nn.sigmoid(a0_ref[:, cols] + _dot(lora16, a2_ref[:, cols]))
        g_o[:, cols] = _dot(sg16, g2_ref[:, cols]).astype(BF16)
        kk = k0 * kk_ref[:, cols]
        kkn = kk * lax.rsqrt(jnp.maximum(head_sum(kk * kk), 1e-24))
        k = k0 * (1.0 + (a - 1.0) * ka_ref[:, cols])
        r_o[:, cols] = r.astype(BF16)
        k_o[:, cols] = k.astype(BF16)
        v_o[:, cols] = v.astype(BF16)
        kkn_o[:, cols] = kkn.astype(BF16)
        b_o[:, cols] = (kkn * a).astype(BF16)
        bonus_o[:, cols] = (head_sum(r * k * rk_ref[:, cols]) * v).astype(BF16)

    def attn_mm(g):
        c = ATT0 + g * sw
        return [mm(c + j * ATTN_WIDTH, c + j * ATTN_WIDTH + sw) for j in range(3)]

    def attn_strip(g, pqkv):
        cols = slice(g * sw, (g + 1) * sw)
        pq, pk, pv = pqkv
        qn = pq * lax.rsqrt(head_sum(pq * pq) * inv_n + RMS_EPS) * qg_ref[:, cols]
        kn = pk * lax.rsqrt(head_sum(pk * pk) * inv_n + RMS_EPS) * kg_ref[:, cols]
        aq_o[:, cols] = qn.astype(BF16)
        ak_o[:, cols] = kn.astype(BF16)
        av_o[:, cols] = pv.astype(BF16)

    def gate_mm(q):
        return mm(GATE0 + q * 2 * sw, GATE0 + (q + 1) * 2 * sw)

    def gate_strip(q, pg):
        cols = slice(q * 2 * sw, (q + 1) * 2 * sw)
        gate_o[:, cols] = jax.nn.sigmoid(pg + bg_ref[:, cols]).astype(BF16)

    p_l = mm(3 * w, RW_PAD)
    nxt = rwkv_mm(0)
    ps_l = shift(p_l, 3 * w, RW_PAD)
    lora = ps_l[:, :LORA_PAD]
    lora16 = lora.astype(BF16)
    tanh16 = jnp.tanh(lora).astype(BF16)
    sg16 = jax.nn.sigmoid(ps_l[:, LORA_PAD:]).astype(BF16)

    n_rw = w // sw
    n_gate = 2 * D_MODEL // (2 * sw)
    for q in range(n_rw):
        cur = nxt
        nxt = rwkv_mm(q + 1) if q + 1 < n_rw else attn_mm(0)
        rwkv_strip(q, cur, lora16, tanh16, sg16)
    for g in range(ATTN_GROUPS):
        cur = nxt
        nxt = attn_mm(g + 1) if g + 1 < ATTN_GROUPS else gate_mm(0)
        attn_strip(g, cur)
    for q in range(n_gate):
        cur = nxt
        nxt = gate_mm(q + 1) if q + 1 < n_gate else None
        gate_strip(q, cur)


def _proj(x, seq_len, gain, w, wag, bg, mu, w0, w2, a0, a2, g2, k_k, k_a, r_k, qg, kg):
    t = x.shape[0]
    tm = PROJ_TILE
    row = lambda n: pl.BlockSpec((tm, n), lambda i: (i, 0))
    wide = jax.ShapeDtypeStruct((t, RWKV_WIDTH), BF16)
    wide32 = jax.ShapeDtypeStruct((t, RWKV_WIDTH), F32)
    att = jax.ShapeDtypeStruct((t, ATTN_WIDTH), BF16)
    return pl.pallas_call(
        functools.partial(_proj_body, seq_len // tm),
        grid=(t // tm,),
        in_specs=[row(D_MODEL)] + [_resident(a.shape) for a in
                                   (gain, w, wag, bg, mu, w0, w2, a0, a2, g2, k_k, k_a, r_k, qg, kg)],
        out_specs=[row(RWKV_WIDTH)] * 8 + [row(ATTN_WIDTH)] * 3 + [row(2 * D_MODEL)],
        out_shape=[wide, wide32] + [wide] * 6 + [att] * 3
                  + [jax.ShapeDtypeStruct((t, 2 * D_MODEL), BF16)],
        scratch_shapes=[pltpu.VMEM((1, RW_PAD), F32)],
        compiler_params=pltpu.CompilerParams(
            dimension_semantics=("arbitrary",), vmem_limit_bytes=VMEM_LIMIT_BYTES),
        name="proj",
    )(x, gain, w, wag, bg, mu, w0, w2, a0, a2, g2, k_k, k_a, r_k, qg, kg)


def _wkv_stages(seqs, r_ref, lw_ref, k_ref, v_ref, kkn_ref, b_ref, y_ref, z_ref):
    c_len = WKV_CHUNK
    n_tiles = RWKV_WIDTH // MXU_DIM

    trow = lax.broadcasted_iota(jnp.int32, (c_len, 1), 0)
    lane = lax.broadcasted_iota(jnp.int32, (c_len, MXU_DIM), 1)
    rowc = lax.broadcasted_iota(jnp.int32, (c_len, MXU_DIM), 0)
    head_mask = [(lane // HEAD_DIM) == h for h in range(HEADS_PER_TILE)]
    blk_mask = [(lane // c_len) == h for h in range(HEADS_PER_TILE)]
    col_in_blk = lane % c_len
    strict = col_in_blk < rowc
    incl = col_in_blk <= rowc
    eye = jnp.where(col_in_blk == rowc, 1.0, 0.0)
    br = lax.broadcasted_iota(jnp.int32, (MXU_DIM, MXU_DIM), 0) // HEAD_DIM
    bc = lax.broadcasted_iota(jnp.int32, (MXU_DIM, MXU_DIM), 1) // HEAD_DIM
    bd_mask = br == bc

    def stack(x, masks):
        return jnp.concatenate([jnp.where(m, x, 0.0) for m in masks], axis=0).astype(BF16)

    def rows(*xs):
        return jnp.concatenate(xs, axis=0)

    units = []
    for s in seqs:
        lw = lw_ref[s]
        cum = lw
        for sh in (1, 2, 4, 8, 16, 32):
            cum = cum + jnp.where(trow >= sh, pltpu.roll(cum, sh, 0), 0.0)
        ref = cum[c_len // 2 - 1:c_len // 2, :]
        e_fwd = jnp.exp(cum - ref)
        e_prev = jnp.exp(cum - lw - ref)
        e_bwd = jnp.exp(ref - cum)
        e_ref = jnp.exp(ref)
        lam_end = e_fwd[c_len - 1:c_len, :]
        lam_c = jnp.exp(cum[c_len - 1:c_len, :])
        at = -kkn_ref[s].astype(F32) * e_prev
        rt = r_ref[s].astype(F32) * e_fwd
        kt = k_ref[s].astype(F32) * e_bwd
        bt = b_ref[s].astype(F32) * e_bwd
        v = v_ref[s].astype(F32)
        for q in range(n_tiles):
            sl = slice(q * MXU_DIM, (q + 1) * MXU_DIM)
            units.append(dict(
                s=s, sl=sl, zi=s * n_tiles + q, at=at[:, sl], rt=rt[:, sl], kt=kt[:, sl],
                bt=bt[:, sl], v=v[:, sl], e_ref=e_ref[:, sl], lam_end=lam_end[:, sl],
                lam_c=lam_c[:, sl]))
        yield

    for u in units:
        ar = rows(u["at"], u["rt"]).astype(BF16)
        u["sb"] = _dot_nt(ar, stack(u["bt"], head_mask))
        u["sk"] = _dot_nt(ar, stack(u["kt"], head_mask))
    yield
    for u in units:
        u["a_ab"] = jnp.where(strict, u["sb"][:c_len], 0.0)
        u["a_rb"] = jnp.where(incl, u["sb"][c_len:], 0.0).astype(BF16)
        a_ak = jnp.where(strict, u["sk"][:c_len], 0.0)
        a_rk = jnp.where(incl, u["sk"][c_len:], 0.0)
        u["xv"] = _dot(rows(a_ak, a_rk).astype(BF16), stack(u["v"], head_mask))
        u["pw"] = _dot(u["a_ab"].astype(BF16), stack(u["a_ab"], blk_mask))
        u["t_inv"] = eye + u["a_ab"]
    yield
    for lev in range(1, WKV_LOG2_CHUNK):
        for u in units:
            rhs = stack(u["pw"], blk_mask)
            if lev < WKV_LOG2_CHUNK - 1:
                prod = _dot(rows(u["pw"], u["t_inv"]).astype(BF16), rhs)
                u["pw"] = prod[:c_len]
                u["t_inv"] = u["t_inv"] + prod[c_len:]
            else:
                u["t_inv"] = u["t_inv"] + _dot(u["t_inv"].astype(BF16), rhs)
        yield
    for u in units:
        z16 = z_ref[u["zi"]].astype(BF16)
        ra = rows(u["rt"] * u["e_ref"], u["at"] * u["e_ref"]).astype(BF16)
        zs = _dot_nt(ra, z16)
        u["ys"] = zs[:c_len]
        u["rhs_u"] = zs[c_len:] + u["xv"][:c_len]
    yield
    for u in units:
        u["uu"] = _dot(u["t_inv"].astype(BF16), stack(u["rhs_u"], head_mask))
    yield
    for u in units:
        y = u["ys"] + _dot(u["a_rb"], stack(u["uu"], head_mask)) + u["xv"][c_len:]
        y_ref[u["s"], :, u["sl"]] = y.astype(BF16)
        bh = (u["bt"] * u["lam_end"]).astype(BF16)
        kh = (u["kt"] * u["lam_end"]).astype(BF16)
        g = _dot_tn(rows(u["uu"], u["v"]).astype(BF16), rows(bh, kh))
        z_ref[u["zi"]] = z_ref[u["zi"]] * u["lam_c"] + jnp.where(bd_mask, g, 0.0)


def _wkv_body(*refs):
    z_ref = refs[-1]
    n_seq = refs[0].shape[0]

    @pl.when(pl.program_id(1) == 0)
    def _():
        z_ref[...] = jnp.zeros_like(z_ref)

    n_groups = min(WKV_GROUPS, n_seq)
    per = n_seq // n_groups
    gens = [_wkv_stages(range(g * per, (g + 1) * per), *refs) for g in range(n_groups)]
    live = [True] * n_groups
    tick = 0
    while any(live):
        for g, gen in enumerate(gens):
            if live[g] and tick >= g * WKV_STAGGER:
                live[g] = next(gen, "done") != "done"
        tick += 1


def _wkv(batch, seq_len, r, lw, k, v, kkn, b):
    n_chunks = seq_len // WKV_CHUNK
    n_seq = WKV_SEQS if batch % WKV_SEQS == 0 else 1
    view = lambda a: a.reshape(batch, seq_len, RWKV_WIDTH)
    blk = pl.BlockSpec((n_seq, WKV_CHUNK, RWKV_WIDTH), lambda bi, ci: (bi, ci, 0))
    y = pl.pallas_call(
        _wkv_body,
        grid=(batch // n_seq, n_chunks),
        in_specs=[blk] * 6,
        out_specs=blk,
        out_shape=jax.ShapeDtypeStruct((batch, seq_len, RWKV_WIDTH), BF16),
        scratch_shapes=[pltpu.VMEM((n_seq * RWKV_WIDTH // MXU_DIM, MXU_DIM, MXU_DIM), F32)],
        compiler_params=pltpu.CompilerParams(
            dimension_semantics=("arbitrary", "arbitrary"), vmem_limit_bytes=VMEM_LIMIT_BYTES),
        name="wkv",
    )(*(view(a) for a in (r, lw, k, v, kkn, b)))
    return y.reshape(batch * seq_len, RWKV_WIDTH)


def _phase_perm(dil):
    n = MXU_DIM // dil
    out = lax.broadcasted_iota(jnp.int32, (MXU_DIM, MXU_DIM), 0)
    src = lax.broadcasted_iota(jnp.int32, (MXU_DIM, MXU_DIM), 1)
    return jnp.where(src == (out % n) * dil + out // n, 1.0, 0.0).astype(BF16)


def _attn_regroup(seq_len, dil, q_ref, k_ref, v_ref, qsub, ksub, vsub):
    blk = ATTN_BLOCK
    sub = seq_len // dil
    n = MXU_DIM // dil
    ksub[0:blk, :] = jnp.zeros((blk, ATTN_GROUP_WIDTH), BF16)
    vsub[0:blk, :] = jnp.zeros((blk, ATTN_GROUP_WIDTH), BF16)
    perm = None if dil == 1 else _phase_perm(dil)
    for src, dst, off in ((q_ref, qsub, 0), (k_ref, ksub, blk), (v_ref, vsub, blk)):
        if dil == 1:
            dst[off:off + seq_len, :] = src[...]
            continue
        for t in range(seq_len // MXU_DIM):
            y = _dot(perm, src[t * MXU_DIM:(t + 1) * MXU_DIM, :]).astype(BF16)
            for c in range(dil):
                dst[off + c * sub + t * n:off + c * sub + (t + 1) * n, :] = y[c * n:(c + 1) * n]


def _attn_scatter(seq_len, dil, osub, lsub, o_ref, l_ref):
    sub = seq_len // dil
    n = MXU_DIM // dil
    if dil == 1:
        o_ref[...] = osub[...]
        l_ref[...] = lsub[...]
        return
    perm = _phase_perm(dil)
    for t in range(seq_len // MXU_DIM):
        rows = lambda ref: jnp.concatenate(
            [ref[c * sub + t * n:c * sub + (t + 1) * n, :] for c in range(dil)], axis=0)
        out = slice(t * MXU_DIM, (t + 1) * MXU_DIM)
        o_ref[out, :] = _dot_tn(perm, rows(osub)).astype(BF16)
        l = rows(lsub)
        hi = l.astype(BF16)
        rem = l - hi.astype(F32)
        mid = rem.astype(BF16)
        lo = (rem - mid.astype(F32)).astype(BF16)
        l_ref[out, :] = _dot_tn(perm, hi) + _dot_tn(perm, mid) + _dot_tn(perm, lo)


def _attn_body(seq_len, q_ref, k_ref, v_ref, o_ref, l_ref, qsub, ksub, vsub, osub, lsub):
    g = pl.program_id(1)
    blk = ATTN_BLOCK
    n_blocks = seq_len // blk

    for gi, (_, dil) in enumerate(ATTN_PAIRS):
        pl.when(g == gi)(functools.partial(
            _attn_regroup, seq_len, dil, q_ref, k_ref, v_ref, qsub, ksub, vsub))

    blocks_per_sub = jnp.int32(n_blocks)
    for gi, (_, dil) in enumerate(ATTN_PAIRS):
        blocks_per_sub = jnp.where(g == gi, n_blocks // dil, blocks_per_sub)

    qi = lax.broadcasted_iota(jnp.int32, (blk, 2 * blk), 0)
    kj = lax.broadcasted_iota(jnp.int32, (blk, 2 * blk), 1)
    dist = qi + blk - kj
    bias_any = jnp.where((dist >= 0) & (dist <= blk), 0.0, NEG_INF)
    bias_first = jnp.where(kj >= blk, bias_any, NEG_INF)
    lane = lax.broadcasted_iota(jnp.int32, (blk, ATTN_GROUP_WIDTH), 1)
    head_mask = [(lane // HEAD_DIM) == h for h in range(HEADS_PER_TILE)]

    def blocks(it, carry):
        ids = [it * ATTN_UNROLL + j for j in range(ATTN_UNROLL)]
        r0s = [pl.multiple_of(i * blk, blk) for i in ids]
        ss, ps, dens, ms, os_ = [], [], [], [], []
        for i, r0 in zip(ids, r0s):
            q = qsub[pl.ds(r0, blk), :]
            qs = jnp.concatenate([jnp.where(m, q, jnp.zeros_like(q)) for m in head_mask], axis=0)
            bias = jnp.where((i % blocks_per_sub) > 0, bias_any, bias_first)
            s_ = _dot_nt(qs, ksub[pl.ds(r0, 2 * blk), :])
            ss.append((s_.reshape(HEADS_PER_TILE, blk, 2 * blk) + bias[None]).reshape(s_.shape))
        for s_ in ss:
            ms.append(jnp.max(s_, axis=-1, keepdims=True))
        for s_, m in zip(ss, ms):
            ps.append(jnp.exp(s_ - m))
        for p in ps:
            dens.append(jnp.sum(p, axis=-1, keepdims=True))
        for p, r0 in zip(ps, r0s):
            os_.append(_dot(p.astype(BF16), vsub[pl.ds(r0, 2 * blk), :]))
        for o_all, den, m, r0 in zip(os_, dens, ms, r0s):
            o_all = o_all * (1.0 / den)
            lse = m + jnp.log(den)
            o = jnp.zeros((blk, ATTN_GROUP_WIDTH), F32)
            l = jnp.zeros((blk, ATTN_GROUP_WIDTH), F32)
            for h in range(HEADS_PER_TILE):
                rows = slice(h * blk, (h + 1) * blk)
                o = jnp.where(head_mask[h], o_all[rows], o)
                l = jnp.where(head_mask[h], lse[rows], l)
            osub[pl.ds(r0, blk), :] = o.astype(BF16)
            lsub[pl.ds(r0, blk), :] = l
        return carry

    lax.fori_loop(0, n_blocks // ATTN_UNROLL, blocks, 0)

    for gi, (_, dil) in enumerate(ATTN_PAIRS):
        pl.when(g == gi)(functools.partial(_attn_scatter, seq_len, dil, osub, lsub, o_ref, l_ref))


def _attn(batch, seq_len, aq, ak, av):
    view = lambda a: a.reshape(batch, seq_len, ATTN_WIDTH)
    spec = pl.BlockSpec((None, seq_len, ATTN_GROUP_WIDTH), lambda bi, g: (bi, 0, g))
    sub_rows = seq_len + ATTN_BLOCK
    o, l = pl.pallas_call(
        functools.partial(_attn_body, seq_len),
        grid=(batch, ATTN_GROUPS),
        in_specs=[spec] * 3,
        out_specs=[spec] * 2,
        out_shape=[jax.ShapeDtypeStruct((batch, seq_len, ATTN_WIDTH), BF16),
                   jax.ShapeDtypeStruct((batch, seq_len, ATTN_WIDTH), F32)],
        scratch_shapes=[
            pltpu.VMEM((seq_len, ATTN_GROUP_WIDTH), BF16),
            pltpu.VMEM((sub_rows, ATTN_GROUP_WIDTH), BF16),
            pltpu.VMEM((sub_rows, ATTN_GROUP_WIDTH), BF16),
            pltpu.VMEM((seq_len, ATTN_GROUP_WIDTH), BF16),
            pltpu.VMEM((seq_len, ATTN_GROUP_WIDTH), F32),
        ],
        compiler_params=pltpu.CompilerParams(
            dimension_semantics=("parallel", "arbitrary"), vmem_limit_bytes=VMEM_LIMIT_BYTES),
        name="attn",
    )(view(aq), view(ak), view(av))
    return o.reshape(batch * seq_len, ATTN_WIDTH), l.reshape(batch * seq_len, ATTN_WIDTH)


def _merge_body(x_ref, wkv_ref, bonus_ref, g_ref, gate_ref, o_ref, l_ref,
                lnw_ref, lnb_ref, ones_ref, wpr_ref, wpa_ref, wo_ref, out_ref):
    ones_bd = ones_ref[...]
    inv_n = 1.0 / HEAD_DIM
    wkv = wkv_ref[...].astype(F32)
    dev = wkv - _head_sums(wkv, ones_bd) * inv_n
    var = _head_sums(dev * dev, ones_bd) * inv_n
    y = dev * lax.rsqrt(var + GN_EPS) * lnw_ref[...] + lnb_ref[...]
    y_a = ((y + bonus_ref[...].astype(F32)) * g_ref[...].astype(F32)).astype(BF16)
    pa = _dot(y_a, wpr_ref[...])
    gw = ATTN_GROUP_WIDTH
    l0, l1, l2 = l_ref[:, 0:gw], l_ref[:, gw:2 * gw], l_ref[:, 2 * gw:3 * gw]
    mx = jnp.maximum(jnp.maximum(l0, l1), l2)
    e0, e1, e2 = jnp.exp(l0 - mx), jnp.exp(l1 - mx), jnp.exp(l2 - mx)
    inv = 1.0 / (e0 + e1 + e2)
    pb = None
    for gi, e in enumerate((e0, e1, e2)):
        yb = (o_ref[:, gi * gw:(gi + 1) * gw].astype(F32) * (e * inv)).astype(BF16)
        part = _dot(yb, wpa_ref[gi * gw:(gi + 1) * gw, :])
        pb = part if pb is None else pb + part
    gates = gate_ref[...].astype(F32)
    merged = gates[:, :D_MODEL] * pa + gates[:, D_MODEL:] * pb
    out_ref[...] = x_ref[...] + _dot(merged.astype(BF16), wo_ref[...])


def _merge(x, wkv, bonus, g, gates, attn_o, attn_l, ln_w, ln_b, ones_bd, wpr, wpa, wo):
    t = x.shape[0]
    tm = MERGE_TILE
    row = lambda n: pl.BlockSpec((tm, n), lambda i: (i, 0))
    small = (ln_w, ln_b, ones_bd, wpr, wpa, wo)
    return pl.pallas_call(
        _merge_body,
        grid=(t // tm,),
        in_specs=[row(D_MODEL)] * 4 + [row(2 * D_MODEL)] + [row(ATTN_WIDTH)] * 2
                 + [_resident(a.shape) for a in small],
        out_specs=row(D_MODEL),
        out_shape=jax.ShapeDtypeStruct((t, D_MODEL), F32),
        compiler_params=pltpu.CompilerParams(
            dimension_semantics=("parallel",), vmem_limit_bytes=VMEM_LIMIT_BYTES),
        name="merge",
    )(x, wkv, bonus, g, gates, attn_o, attn_l, *small)


def _pad_rows(a, n):
    return jnp.pad(a, ((0, n - a.shape[0]), (0, 0)))


def _layer(x, seq_len, ffn1_norm, ffn1_w_in, ffn1_w_out, mix_norm, w_in, b_gate, rwkv_mu,
           rwkv_w0, rwkv_w2, rwkv_a0, rwkv_a2, rwkv_g2, rwkv_k_k, rwkv_k_a, rwkv_r_k,
           rwkv_ln_w, rwkv_ln_b, attn_q_norm, attn_k_norm, w_proj_rwkv, w_proj_attn,
           w_out, ffn2_norm, ffn2_w_in, ffn2_w_out):
    batch = x.shape[0] // seq_len
    row = lambda a: a.reshape(1, -1)

    w_rw = jnp.pad(w_in[:, :RWKV_COLS].astype(BF16), ((0, 0), (0, RW_PAD - RWKV_COLS)))
    w_ag = w_in[:, RWKV_COLS:].astype(BF16)
    mu_pad = jnp.pad(rwkv_mu, (0, RW_PAD - RWKV_COLS)).reshape(1, RW_PAD)
    w2_pad = _pad_rows(rwkv_w2, LORA_PAD).astype(BF16)
    a2_pad = jnp.pad(rwkv_a2, ((DECAY_LORA, 0), (0, 0))).astype(BF16)
    g2_pad = _pad_rows(rwkv_g2, GATE_LORA_PAD).astype(BF16)
    n_heads = ATTN_WIDTH // HEAD_DIM
    q_gain = row(jnp.tile(attn_q_norm, n_heads) * HEAD_DIM ** -0.5)
    k_gain = row(jnp.tile(attn_k_norm, n_heads))
    idx = jnp.arange(MXU_DIM) // HEAD_DIM
    ones_bd = (idx[:, None] == idx[None, :]).astype(BF16)

    x = _ffn(x, row(ffn1_norm), ffn1_w_in.astype(BF16), ffn1_w_out.astype(BF16))
    (r, lw, k, v, kkn, b, bonus, g, aq, ak, av, gates) = _proj(
        x, seq_len, row(mix_norm), w_rw, w_ag, row(b_gate), mu_pad, row(rwkv_w0), w2_pad,
        row(rwkv_a0), a2_pad, g2_pad, row(rwkv_k_k), row(rwkv_k_a), row(rwkv_r_k),
        q_gain, k_gain)
    wkv = _wkv(batch, seq_len, r, lw, k, v, kkn, b)
    attn_o, attn_l = _attn(batch, seq_len, aq, ak, av)
    x = _merge(x, wkv, bonus, g, gates, attn_o, attn_l, row(rwkv_ln_w), row(rwkv_ln_b), ones_bd,
               w_proj_rwkv.astype(BF16), w_proj_attn.astype(BF16), w_out.astype(BF16))
    return _ffn(x, row(ffn2_norm), ffn2_w_in.astype(BF16), ffn2_w_out.astype(BF16))


def kernel(x, ffn1_norm, ffn1_w_in, ffn1_w_out, mix_norm, w_in, b_gate, rwkv_mu, rwkv_w0, rwkv_w2, rwkv_a0, rwkv_a2, rwkv_g2, rwkv_k_k, rwkv_k_a, rwkv_r_k, rwkv_ln_w, rwkv_ln_b, attn_q_norm, attn_k_norm, w_proj_rwkv, w_proj_attn, w_out, ffn2_norm, ffn2_w_in, ffn2_w_out):
    batch, seq_len, d = x.shape
    params = (ffn1_norm, ffn1_w_in, ffn1_w_out, mix_norm, w_in, b_gate, rwkv_mu, rwkv_w0, rwkv_w2,
              rwkv_a0, rwkv_a2, rwkv_g2, rwkv_k_k, rwkv_k_a, rwkv_r_k, rwkv_ln_w, rwkv_ln_b,
              attn_q_norm, attn_k_norm, w_proj_rwkv, w_proj_attn, w_out, ffn2_norm, ffn2_w_in,
              ffn2_w_out)
    h = x.reshape(batch * seq_len, d)
    for layer in range(ffn1_norm.shape[0]):
        h = _layer(h, seq_len, *(p[layer] for p in params))
    return h.reshape(batch, seq_len, d)
```

```python
import functools

import jax
import jax.numpy as jnp
from jax import lax
from jax.experimental import pallas as pl
from jax.experimental.pallas import tpu as pltpu

F32 = jnp.float32
BF16 = jnp.bfloat16

D_MODEL = 1024
HEAD_DIM = 64
RWKV_WIDTH = 1024
DECAY_LORA = 64
AAA_LORA = 64
GATE_LORA = 160
RWKV_COLS = 3 * RWKV_WIDTH + DECAY_LORA + AAA_LORA + GATE_LORA
GN_EPS = 64e-5
ATTN_PAIRS = ((128, 1), (512, 4), (2048, 16))
ATTN_GROUPS = 3
ATTN_GROUP_WIDTH = 256
ATTN_WIDTH = ATTN_GROUPS * ATTN_GROUP_WIDTH
ATTN_BLOCK = 128
D_FF = 2816
RMS_EPS = 1e-6
NEG_INF = -1e30

LANES = 128
SUBLANES = 8
MXU_DIM = 256
HEADS_PER_TILE = MXU_DIM // HEAD_DIM
VMEM_LIMIT_BYTES = 56 * 1024 * 1024

LORA_PAD = 128
GATE_LORA_PAD = 256
RW_PAD = 3 * RWKV_WIDTH + LORA_PAD + GATE_LORA_PAD
ATT0 = RW_PAD
GATE0 = ATT0 + 3 * ATTN_WIDTH

FF_CHUNK = MXU_DIM
FFN_TILE = 512
PROJ_TILE = 256
MERGE_TILE = 512
WKV_CHUNK = 64
WKV_LOG2_CHUNK = 6
ATTN_UNROLL = 4
WKV_SEQS = 8
WKV_GROUPS = 8
WKV_STAGGER = 1


def _dot(a, b):
    return jnp.dot(a, b, preferred_element_type=F32)


def _dot_nt(a, b):
    return lax.dot_general(a, b, (((1,), (1,)), ((), ())), preferred_element_type=F32)


def _dot_tn(a, b):
    return lax.dot_general(a, b, (((0,), (0,)), ((), ())), preferred_element_type=F32)


def _rms_norm_bf16(x, gain):
    ms = jnp.mean(x * x, axis=-1, keepdims=True)
    return (x * lax.rsqrt(ms + RMS_EPS) * gain).astype(BF16)


def _head_sums(t, ones_bd):
    parts = []
    for q in range(t.shape[-1] // MXU_DIM):
        parts.append(_dot(t[:, q * MXU_DIM:(q + 1) * MXU_DIM].astype(BF16), ones_bd))
    return jnp.concatenate(parts, axis=-1)


def _head_sums_xlane(t):
    low_half = lax.broadcasted_iota(jnp.int32, (1, LANES), 1) < HEAD_DIM
    parts = []
    for j in range(t.shape[-1] // LANES):
        ts = t[:, j * LANES:(j + 1) * LANES]
        s0 = jnp.sum(jnp.where(low_half, ts, 0.0), axis=-1, keepdims=True)
        s1 = jnp.sum(jnp.where(low_half, 0.0, ts), axis=-1, keepdims=True)
        parts.append(jnp.where(low_half, s0, s1))
    return jnp.concatenate(parts, axis=-1)


def _resident(shape):
    return pl.BlockSpec(shape, lambda *_: (0,) * len(shape), pipeline_mode=pl.Buffered(1))


def _ffn_body(x_ref, gain_ref, w_in_ref, w_out_ref, o_ref):
    x = x_ref[...]
    h = _rms_norm_bf16(x, gain_ref[...])
    acc = None
    for c in range(D_FF // FF_CHUNK):
        lo = c * FF_CHUNK
        gate = _dot(h, w_in_ref[:, lo:lo + FF_CHUNK])
        up = _dot(h, w_in_ref[:, D_FF + lo:D_FF + lo + FF_CHUNK])
        act = (gate * jax.nn.sigmoid(gate) * up).astype(BF16)
        part = _dot(act, w_out_ref[lo:lo + FF_CHUNK, :])
        acc = part if acc is None else acc + part
    o_ref[...] = x + 0.5 * acc


def _ffn(x, gain, w_in, w_out):
    t = x.shape[0]
    tm = FFN_TILE
    return pl.pallas_call(
        _ffn_body,
        grid=(t // tm,),
        in_specs=[
            pl.BlockSpec((tm, D_MODEL), lambda i: (i, 0)),
            _resident((1, D_MODEL)),
            _resident((D_MODEL, 2 * D_FF)),
            _resident((D_FF, D_MODEL)),
        ],
        out_specs=pl.BlockSpec((tm, D_MODEL), lambda i: (i, 0)),
        out_shape=jax.ShapeDtypeStruct((t, D_MODEL), F32),
        compiler_params=pltpu.CompilerParams(
            dimension_semantics=("parallel",), vmem_limit_bytes=VMEM_LIMIT_BYTES),
        name="ffn",
    )(x, gain, w_in, w_out)


def _softplus(z):
    return jnp.maximum(z, 0.0) + jnp.log(1.0 + jnp.exp(-jnp.abs(z)))


def _proj_body(tiles_per_seq,
               x_ref, gain_ref, w_ref, wag_ref, bg_ref, mu_ref, w0_ref, w2_ref, a0_ref, a2_ref, g2_ref,
               kk_ref, ka_ref, rk_ref, qg_ref, kg_ref,
               r_o, lw_o, k_o, v_o, kkn_o, b_o, bonus_o, g_o, aq_o, ak_o, av_o, gate_o,
               carry_ref):
    i = pl.program_id(0)
    tm = x_ref.shape[0]
    w = RWKV_WIDTH
    sw = MXU_DIM
    h = _rms_norm_bf16(x_ref[...], gain_ref[...])
    row8 = lax.broadcasted_iota(jnp.int32, (SUBLANES, 1), 0)
    inv_n = 1.0 / HEAD_DIM

    @pl.when((i % tiles_per_seq) == 0)
    def _():
        carry_ref[...] = jnp.zeros_like(carry_ref)

    def mm(lo, hi):
        if hi <= ATT0:
            return _dot(h, w_ref[:, lo:hi])
        return _dot(h, wag_ref[:, lo - ATT0:hi - ATT0])

    def shift(p, lo, hi):
        rolled = pltpu.roll(p, 1, 0)
        head = jnp.where(row8 == 0, carry_ref[:, lo:hi], rolled[:SUBLANES])
        prev = jnp.concatenate([head, rolled[SUBLANES:]], axis=0)
        carry_ref[:, lo:hi] = p[tm - 1:tm, :]
        return p + (prev - p) * mu_ref[:, lo:hi]

    head_sum = _head_sums_xlane

    def rwkv_mm(q):
        c = q * sw
        return [mm(j * w + c, j * w + c + sw) for j in range(3)]

    def rwkv_strip(q, prk, lora16, tanh16, sg16):
        c = q * sw
        cols = slice(c, c + sw)
        r = shift(prk[0], c, c + sw)
        k0 = shift(prk[1], w + c, w + c + sw)
        v = shift(prk[2], 2 * w + c, 2 * w + c + sw)
        wraw = w0_ref[:, cols] + _dot(tanh16, w2_ref[:, cols])
        lw_o[:, cols] = -jnp.exp(-_softplus(-wraw) - 0.5)
        a = jax.nn.sigmoid(a0_ref[:, cols] + _dot(lora16, a2_ref[:, cols]))
        g_o[:, cols] = _dot(sg16, g2_ref[:, cols]).astype(BF16)
        kk = k0 * kk_ref[:, cols]
        kkn = kk * lax.rsqrt(jnp.maximum(head_sum(kk * kk), 1e-24))
        k = k0 * (1.0 + (a - 1.0) * ka_ref[:, cols])
        r_o[:, cols] = r.astype(BF16)
        k_o[:, cols] = k.astype(BF16)
        v_o[:, cols] = v.astype(BF16)
        kkn_o[:, cols] = kkn.astype(BF16)
        b_o[:, cols] = (kkn * a).astype(BF16)
        bonus_o[:, cols] = (head_sum(r * k * rk_ref[:, cols]) * v).astype(BF16)

    def attn_mm(g):
        c = ATT0 + g * sw
        return [mm(c + j * ATTN_WIDTH, c + j * ATTN_WIDTH + sw) for j in range(3)]

    def attn_strip(g, pqkv):
        cols = slice(g * sw, (g + 1) * sw)
        pq, pk, pv = pqkv
        qn = pq * lax.rsqrt(head_sum(pq * pq) * inv_n + RMS_EPS) * qg_ref[:, cols]
        kn = pk * lax.rsqrt(head_sum(pk * pk) * inv_n + RMS_EPS) * kg_ref[:, cols]
        aq_o[:, cols] = qn.astype(BF16)
        ak_o[:, cols] = kn.astype(BF16)
        av_o[:, cols] = pv.astype(BF16)

    def gate_mm(q):
        return mm(GATE0 + q * 2 * sw, GATE0 + (q + 1) * 2 * sw)

    def gate_strip(q, pg):
        cols = slice(q * 2 * sw, (q + 1) * 2 * sw)
        gate_o[:, cols] = jax.nn.sigmoid(pg + bg_ref[:, cols]).astype(BF16)

    p_l = mm(3 * w, RW_PAD)
    nxt = rwkv_mm(0)
    ps_l = shift(p_l, 3 * w, RW_PAD)
    lora = ps_l[:, :LORA_PAD]
    lora16 = lora.astype(BF16)
    tanh16 = jnp.tanh(lora).astype(BF16)
    sg16 = jax.nn.sigmoid(ps_l[:, LORA_PAD:]).astype(BF16)

    n_rw = w // sw
    n_gate = 2 * D_MODEL // (2 * sw)
    for q in range(n_rw):
        cur = nxt
        nxt = rwkv_mm(q + 1) if q + 1 < n_rw else attn_mm(0)
        rwkv_strip(q, cur, lora16, tanh16, sg16)
    for g in range(ATTN_GROUPS):
        cur = nxt
        nxt = attn_mm(g + 1) if g + 1 < ATTN_GROUPS else gate_mm(0)
        attn_strip(g, cur)
    for q in range(n_gate):
        cur = nxt
        nxt = gate_mm(q + 1) if q + 1 < n_gate else None
        gate_strip(q, cur)


def _proj(x, seq_len, gain, w, wag, bg, mu, w0, w2, a0, a2, g2, k_k, k_a, r_k, qg, kg):
    t = x.shape[0]
    tm = PROJ_TILE
    row = lambda n: pl.BlockSpec((tm, n), lambda i: (i, 0))
    wide = jax.ShapeDtypeStruct((t, RWKV_WIDTH), BF16)
    wide32 = jax.ShapeDtypeStruct((t, RWKV_WIDTH), F32)
    att = jax.ShapeDtypeStruct((t, ATTN_WIDTH), BF16)
    return pl.pallas_call(
        functools.partial(_proj_body, seq_len // tm),
        grid=(t // tm,),
        in_specs=[row(D_MODEL)] + [_resident(a.shape) for a in
                                   (gain, w, wag, bg, mu, w0, w2, a0, a2, g2, k_k, k_a, r_k, qg, kg)],
        out_specs=[row(RWKV_WIDTH)] * 8 + [row(ATTN_WIDTH)] * 3 + [row(2 * D_MODEL)],
        out_shape=[wide, wide32] + [wide] * 6 + [att] * 3
                  + [jax.ShapeDtypeStruct((t, 2 * D_MODEL), BF16)],
        scratch_shapes=[pltpu.VMEM((1, RW_PAD), F32)],
        compiler_params=pltpu.CompilerParams(
            dimension_semantics=("arbitrary",), vmem_limit_bytes=VMEM_LIMIT_BYTES),
        name="proj",
    )(x, gain, w, wag, bg, mu, w0, w2, a0, a2, g2, k_k, k_a, r_k, qg, kg)


def _wkv_stages(seqs, r_ref, lw_ref, k_ref, v_ref, kkn_ref, b_ref, y_ref, z_ref):
    c_len = WKV_CHUNK
    n_tiles = RWKV_WIDTH // MXU_DIM

    trow = lax.broadcasted_iota(jnp.int32, (c_len, 1), 0)
    lane = lax.broadcasted_iota(jnp.int32, (c_len, MXU_DIM), 1)
    rowc = lax.broadcasted_iota(jnp.int32, (c_len, MXU_DIM), 0)
    head_mask = [(lane // HEAD_DIM) == h for h in range(HEADS_PER_TILE)]
    blk_mask = [(lane // c_len) == h for h in range(HEADS_PER_TILE)]
    col_in_blk = lane % c_len
    strict = col_in_blk < rowc
    incl = col_in_blk <= rowc
    eye = jnp.where(col_in_blk == rowc, 1.0, 0.0)

    def stack(x, masks):
        return jnp.concatenate([jnp.where(m, x, 0.0) for m in masks], axis=0).astype(BF16)

    def rows(*xs):
        return jnp.concatenate(xs, axis=0)

    units = []
    for s in seqs:
        lw = lw_ref[s]
        cum = lw
        for sh in (1, 2, 4, 8, 16, 32):
            cum = cum + jnp.where(trow >= sh, pltpu.roll(cum, sh, 0), 0.0)
        ref = cum[c_len // 2 - 1:c_len // 2, :]
        e_fwd = jnp.exp(cum - ref)
        e_prev = jnp.exp(cum - lw - ref)
        e_bwd = jnp.exp(ref - cum)
        e_ref = jnp.exp(ref)
        lam_end = e_fwd[c_len - 1:c_len, :]
        lam_c = jnp.exp(cum[c_len - 1:c_len, :])
        at = -kkn_ref[s].astype(F32) * e_prev
        rt = r_ref[s].astype(F32) * e_fwd
        kt = k_ref[s].astype(F32) * e_bwd
        bt = b_ref[s].astype(F32) * e_bwd
        v = v_ref[s].astype(F32)
        for q in range(n_tiles):
            sl = slice(q * MXU_DIM, (q + 1) * MXU_DIM)
            units.append(dict(
                s=s, sl=sl, zi=s * n_tiles + q, at=at[:, sl], rt=rt[:, sl], kt=kt[:, sl],
                bt=bt[:, sl], v=v[:, sl], e_ref=e_ref[:, sl], lam_end=lam_end[:, sl],
                lam_c=lam_c[:, sl]))
        yield

    for u in units:
        ar = rows(u["at"], u["rt"]).astype(BF16)
        u["sb"] = _dot_nt(ar, stack(u["bt"], head_mask))
        u["sk"] = _dot_nt(ar, stack(u["kt"], head_mask))
    yield
    for u in units:
        u["a_ab"] = jnp.where(strict, u["sb"][:c_len], 0.0)
        u["a_rb"] = jnp.where(incl, u["sb"][c_len:], 0.0).astype(BF16)
        a_ak = jnp.where(strict, u["sk"][:c_len], 0.0)
        a_rk = jnp.where(incl, u["sk"][c_len:], 0.0)
        u["xv"] = _dot(rows(a_ak, a_rk).astype(BF16), stack(u["v"], head_mask))
        u["pw"] = _dot(u["a_ab"].astype(BF16), stack(u["a_ab"], blk_mask))
        u["t_inv"] = eye + u["a_ab"]
    yield
    for lev in range(1, WKV_LOG2_CHUNK):
        for u in units:
            rhs = stack(u["pw"], blk_mask)
            if lev < WKV_LOG2_CHUNK - 1:
                prod = _dot(rows(u["pw"], u["t_inv"]).astype(BF16), rhs)
                u["pw"] = prod[:c_len]
                u["t_inv"] = u["t_inv"] + prod[c_len:]
            else:
                u["t_inv"] = u["t_inv"] + _dot(u["t_inv"].astype(BF16), rhs)
        yield
    for u in units:
        ra = rows(u["rt"] * u["e_ref"], u["at"] * u["e_ref"]).astype(BF16)
        zs = _dot_nt(ra, stack(z_ref[u["zi"]], head_mask))
        u["ys"] = zs[:c_len]
        u["rhs_u"] = zs[c_len:] + u["xv"][:c_len]
    yield
    for u in units:
        u["uu"] = _dot(u["t_inv"].astype(BF16), stack(u["rhs_u"], head_mask))
    yield
    for u in units:
        y = u["ys"] + _dot(u["a_rb"], stack(u["uu"], head_mask)) + u["xv"][c_len:]
        y_ref[u["s"], :, u["sl"]] = y.astype(BF16)
        bh = (u["bt"] * u["lam_end"]).astype(BF16)
        kh = (u["kt"] * u["lam_end"]).astype(BF16)
        g = _dot_tn(rows(u["uu"], u["v"]).astype(BF16), rows(bh, kh))
        z_new = z_ref[u["zi"]] * u["lam_c"]
        for h in range(HEADS_PER_TILE):
            z_new = z_new + jnp.where(head_mask[h], g[h * HEAD_DIM:(h + 1) * HEAD_DIM], 0.0)
        z_ref[u["zi"]] = z_new


def _wkv_body(*refs):
    z_ref = refs[-1]
    n_seq = refs[0].shape[0]

    @pl.when(pl.program_id(1) == 0)
    def _():
        z_ref[...] = jnp.zeros_like(z_ref)

    n_groups = min(WKV_GROUPS, n_seq)
    per = n_seq // n_groups
    gens = [_wkv_stages(range(g * per, (g + 1) * per), *refs) for g in range(n_groups)]
    live = [True] * n_groups
    tick = 0
    while any(live):
        for g, gen in enumerate(gens):
            if live[g] and tick >= g * WKV_STAGGER:
                live[g] = next(gen, "done") != "done"
        tick += 1


def _wkv(batch, seq_len, r, lw, k, v, kkn, b):
    n_chunks = seq_len // WKV_CHUNK
    n_seq = WKV_SEQS if batch % WKV_SEQS == 0 else 1
    view = lambda a: a.reshape(batch, seq_len, RWKV_WIDTH)
    blk = pl.BlockSpec((n_seq, WKV_CHUNK, RWKV_WIDTH), lambda bi, ci: (bi, ci, 0))
    y = pl.pallas_call(
        _wkv_body,
        grid=(batch // n_seq, n_chunks),
        in_specs=[blk] * 6,
        out_specs=blk,
        out_shape=jax.ShapeDtypeStruct((batch, seq_len, RWKV_WIDTH), BF16),
        scratch_shapes=[pltpu.VMEM((n_seq * RWKV_WIDTH // MXU_DIM, HEAD_DIM, MXU_DIM), F32)],
        compiler_params=pltpu.CompilerParams(
            dimension_semantics=("arbitrary", "arbitrary"), vmem_limit_bytes=VMEM_LIMIT_BYTES),
        name="wkv",
    )(*(view(a) for a in (r, lw, k, v, kkn, b)))
    return y.reshape(batch * seq_len, RWKV_WIDTH)


def _phase_perm(dil):
    n = MXU_DIM // dil
    out = lax.broadcasted_iota(jnp.int32, (MXU_DIM, MXU_DIM), 0)
    src = lax.broadcasted_iota(jnp.int32, (MXU_DIM, MXU_DIM), 1)
    return jnp.where(src == (out % n) * dil + out // n, 1.0, 0.0).astype(BF16)


def _attn_regroup(seq_len, dil, q_ref, k_ref, v_ref, qsub, ksub, vsub):
    blk = ATTN_BLOCK
    sub = seq_len // dil
    n = MXU_DIM // dil
    ksub[0:blk, :] = jnp.zeros((blk, ATTN_GROUP_WIDTH), BF16)
    vsub[0:blk, :] = jnp.zeros((blk, ATTN_GROUP_WIDTH), BF16)
    perm = None if dil == 1 else _phase_perm(dil)
    for src, dst, off in ((q_ref, qsub, 0), (k_ref, ksub, blk), (v_ref, vsub, blk)):
        if dil == 1:
            dst[off:off + seq_len, :] = src[...]
            continue
        for t in range(seq_len // MXU_DIM):
            y = _dot(perm, src[t * MXU_DIM:(t + 1) * MXU_DIM, :]).astype(BF16)
            for c in range(dil):
                dst[off + c * sub + t * n:off + c * sub + (t + 1) * n, :] = y[c * n:(c + 1) * n]


def _attn_scatter(seq_len, dil, osub, lsub, o_ref, l_ref):
    sub = seq_len // dil
    n = MXU_DIM // dil
    if dil == 1:
        o_ref[...] = osub[...]
        l_ref[...] = lsub[...]
        return
    perm = _phase_perm(dil)
    for t in range(seq_len // MXU_DIM):
        rows = lambda ref: jnp.concatenate(
            [ref[c * sub + t * n:c * sub + (t + 1) * n, :] for c in range(dil)], axis=0)
        out = slice(t * MXU_DIM, (t + 1) * MXU_DIM)
        o_ref[out, :] = _dot_tn(perm, rows(osub)).astype(BF16)
        l = rows(lsub)
        hi = l.astype(BF16)
        rem = l - hi.astype(F32)
        mid = rem.astype(BF16)
        lo = (rem - mid.astype(F32)).astype(BF16)
        l_ref[out, :] = _dot_tn(perm, hi) + _dot_tn(perm, mid) + _dot_tn(perm, lo)


def _attn_body(seq_len, q_ref, k_ref, v_ref, o_ref, l_ref, qsub, ksub, vsub, osub, lsub):
    g = pl.program_id(1)
    blk = ATTN_BLOCK
    n_blocks = seq_len // blk

    for gi, (_, dil) in enumerate(ATTN_PAIRS):
        pl.when(g == gi)(functools.partial(
            _attn_regroup, seq_len, dil, q_ref, k_ref, v_ref, qsub, ksub, vsub))

    blocks_per_sub = jnp.int32(n_blocks)
    for gi, (_, dil) in enumerate(ATTN_PAIRS):
        blocks_per_sub = jnp.where(g == gi, n_blocks // dil, blocks_per_sub)

    qi = lax.broadcasted_iota(jnp.int32, (blk, 2 * blk), 0)
    kj = lax.broadcasted_iota(jnp.int32, (blk, 2 * blk), 1)
    dist = qi + blk - kj
    bias_any = jnp.where((dist >= 0) & (dist <= blk), 0.0, NEG_INF)
    bias_first = jnp.where(kj >= blk, bias_any, NEG_INF)
    lane = lax.broadcasted_iota(jnp.int32, (blk, ATTN_GROUP_WIDTH), 1)
    head_mask = [(lane // HEAD_DIM) == h for h in range(HEADS_PER_TILE)]

    def blocks(it, carry):
        ids = [it * ATTN_UNROLL + j for j in range(ATTN_UNROLL)]
        r0s = [pl.multiple_of(i * blk, blk) for i in ids]
        ss, ps, dens, ms, os_ = [], [], [], [], []
        for i, r0 in zip(ids, r0s):
            q = qsub[pl.ds(r0, blk), :]
            qs = jnp.concatenate([jnp.where(m, q, jnp.zeros_like(q)) for m in head_mask], axis=0)
            bias = jnp.where((i % blocks_per_sub) > 0, bias_any, bias_first)
            s_ = _dot_nt(qs, ksub[pl.ds(r0, 2 * blk), :])
            ss.append((s_.reshape(HEADS_PER_TILE, blk, 2 * blk) + bias[None]).reshape(s_.shape))
        for s_ in ss:
            ms.append(jnp.max(s_, axis=-1, keepdims=True))
        for s_, m in zip(ss, ms):
            ps.append(jnp.exp(s_ - m))
        for p in ps:
            dens.append(jnp.sum(p, axis=-1, keepdims=True))
        for p, r0 in zip(ps, r0s):
            os_.append(_dot(p.astype(BF16), vsub[pl.ds(r0, 2 * blk), :]))
        for o_all, den, m, r0 in zip(os_, dens, ms, r0s):
            o_all = o_all * (1.0 / den)
            lse = m + jnp.log(den)
            o = jnp.zeros((blk, ATTN_GROUP_WIDTH), F32)
            l = jnp.zeros((blk, ATTN_GROUP_WIDTH), F32)
            for h in range(HEADS_PER_TILE):
                rows = slice(h * blk, (h + 1) * blk)
                o = jnp.where(head_mask[h], o_all[rows], o)
                l = jnp.where(head_mask[h], lse[rows], l)
            osub[pl.ds(r0, blk), :] = o.astype(BF16)
            lsub[pl.ds(r0, blk), :] = l
        return carry

    lax.fori_loop(0, n_blocks // ATTN_UNROLL, blocks, 0)

    for gi, (_, dil) in enumerate(ATTN_PAIRS):
        pl.when(g == gi)(functools.partial(_attn_scatter, seq_len, dil, osub, lsub, o_ref, l_ref))


def _attn(batch, seq_len, aq, ak, av):
    view = lambda a: a.reshape(batch, seq_len, ATTN_WIDTH)
    spec = pl.BlockSpec((None, seq_len, ATTN_GROUP_WIDTH), lambda bi, g: (bi, 0, g))
    sub_rows = seq_len + ATTN_BLOCK
    o, l = pl.pallas_call(
        functools.partial(_attn_body, seq_len),
        grid=(batch, ATTN_GROUPS),
        in_specs=[spec] * 3,
        out_specs=[spec] * 2,
        out_shape=[jax.ShapeDtypeStruct((batch, seq_len, ATTN_WIDTH), BF16),
                   jax.ShapeDtypeStruct((batch, seq_len, ATTN_WIDTH), F32)],
        scratch_shapes=[
            pltpu.VMEM((seq_len, ATTN_GROUP_WIDTH), BF16),
            pltpu.VMEM((sub_rows, ATTN_GROUP_WIDTH), BF16),
            pltpu.VMEM((sub_rows, ATTN_GROUP_WIDTH), BF16),
            pltpu.VMEM((seq_len, ATTN_GROUP_WIDTH), BF16),
            pltpu.VMEM((seq_len, ATTN_GROUP_WIDTH), F32),
        ],
        compiler_params=pltpu.CompilerParams(
            dimension_semantics=("parallel", "arbitrary"), vmem_limit_bytes=VMEM_LIMIT_BYTES),
        name="attn",
    )(view(aq), view(ak), view(av))
    return o.reshape(batch * seq_len, ATTN_WIDTH), l.reshape(batch * seq_len, ATTN_WIDTH)


def _merge_body(x_ref, wkv_ref, bonus_ref, g_ref, gate_ref, o_ref, l_ref,
                lnw_ref, lnb_ref, ones_ref, wpr_ref, wpa_ref, wo_ref, out_ref):
    ones_bd = ones_ref[...]
    inv_n = 1.0 / HEAD_DIM
    wkv = wkv_ref[...].astype(F32)
    dev = wkv - _head_sums(wkv, ones_bd) * inv_n
    var = _head_sums(dev * dev, ones_bd) * inv_n
    y = dev * lax.rsqrt(var + GN_EPS) * lnw_ref[...] + lnb_ref[...]
    y_a = ((y + bonus_ref[...].astype(F32)) * g_ref[...].astype(F32)).astype(BF16)
    pa = _dot(y_a, wpr_ref[...])
    gw = ATTN_GROUP_WIDTH
    l0, l1, l2 = l_ref[:, 0:gw], l_ref[:, gw:2 * gw], l_ref[:, 2 * gw:3 * gw]
    mx = jnp.maximum(jnp.maximum(l0, l1), l2)
    e0, e1, e2 = jnp.exp(l0 - mx), jnp.exp(l1 - mx), jnp.exp(l2 - mx)
    inv = 1.0 / (e0 + e1 + e2)
    pb = None
    for gi, e in enumerate((e0, e1, e2)):
        yb = (o_ref[:, gi * gw:(gi + 1) * gw].astype(F32) * (e * inv)).astype(BF16)
        part = _dot(yb, wpa_ref[gi * gw:(gi + 1) * gw, :])
        pb = part if pb is None else pb + part
    gates = gate_ref[...].astype(F32)
    merged = gates[:, :D_MODEL] * pa + gates[:, D_MODEL:] * pb
    out_ref[...] = x_ref[...] + _dot(merged.astype(BF16), wo_ref[...])


def _merge(x, wkv, bonus, g, gates, attn_o, attn_l, ln_w, ln_b, ones_bd, wpr, wpa, wo):
    t = x.shape[0]
    tm = MERGE_TILE
    row = lambda n: pl.BlockSpec((tm, n), lambda i: (i, 0))
    small = (ln_w, ln_b, ones_bd, wpr, wpa, wo)
    return pl.pallas_call(
        _merge_body,
        grid=(t // tm,),
        in_specs=[row(D_MODEL)] * 4 + [row(2 * D_MODEL)] + [row(ATTN_WIDTH)] * 2
                 + [_resident(a.shape) for a in small],
        out_specs=row(D_MODEL),
        out_shape=jax.ShapeDtypeStruct((t, D_MODEL), F32),
        compiler_params=pltpu.CompilerParams(
            dimension_semantics=("parallel",), vmem_limit_bytes=VMEM_LIMIT_BYTES),
        name="merge",
    )(x, wkv, bonus, g, gates, attn_o, attn_l, *small)


def _pad_rows(a, n):
    return jnp.pad(a, ((0, n - a.shape[0]), (0, 0)))


def _layer(x, seq_len, ffn1_norm, ffn1_w_in, ffn1_w_out, mix_norm, w_in, b_gate, rwkv_mu,
           rwkv_w0, rwkv_w2, rwkv_a0, rwkv_a2, rwkv_g2, rwkv_k_k, rwkv_k_a, rwkv_r_k,
           rwkv_ln_w, rwkv_ln_b, attn_q_norm, attn_k_norm, w_proj_rwkv, w_proj_attn,
           w_out, ffn2_norm, ffn2_w_in, ffn2_w_out):
    batch = x.shape[0] // seq_len
    row = lambda a: a.reshape(1, -1)

    w_rw = jnp.pad(w_in[:, :RWKV_COLS].astype(BF16), ((0, 0), (0, RW_PAD - RWKV_COLS)))
    w_ag = w_in[:, RWKV_COLS:].astype(BF16)
    mu_pad = jnp.pad(rwkv_mu, (0, RW_PAD - RWKV_COLS)).reshape(1, RW_PAD)
    w2_pad = _pad_rows(rwkv_w2, LORA_PAD).astype(BF16)
    a2_pad = jnp.pad(rwkv_a2, ((DECAY_LORA, 0), (0, 0))).astype(BF16)
    g2_pad = _pad_rows(rwkv_g2, GATE_LORA_PAD).astype(BF16)
    n_heads = ATTN_WIDTH // HEAD_DIM
    q_gain = row(jnp.tile(attn_q_norm, n_heads) * HEAD_DIM ** -0.5)
    k_gain = row(jnp.tile(attn_k_norm, n_heads))
    idx = jnp.arange(MXU_DIM) // HEAD_DIM
    ones_bd = (idx[:, None] == idx[None, :]).astype(BF16)

    x = _ffn(x, row(ffn1_norm), ffn1_w_in.astype(BF16), ffn1_w_out.astype(BF16))
    (r, lw, k, v, kkn, b, bonus, g, aq, ak, av, gates) = _proj(
        x, seq_len, row(mix_norm), w_rw, w_ag, row(b_gate), mu_pad, row(rwkv_w0), w2_pad,
        row(rwkv_a0), a2_pad, g2_pad, row(rwkv_k_k), row(rwkv_k_a), row(rwkv_r_k),
        q_gain, k_gain)
    wkv = _wkv(batch, seq_len, r, lw, k, v, kkn, b)
    attn_o, attn_l = _attn(batch, seq_len, aq, ak, av)
    x = _merge(x, wkv, bonus, g, gates, attn_o, attn_l, row(rwkv_ln_w), row(rwkv_ln_b), ones_bd,
               w_proj_rwkv.astype(BF16), w_proj_attn.astype(BF16), w_out.astype(BF16))
    return _ffn(x, row(ffn2_norm), ffn2_w_in.astype(BF16), ffn2_w_out.astype(BF16))


def kernel(x, ffn1_norm, ffn1_w_in, ffn1_w_out, mix_norm, w_in, b_gate, rwkv_mu, rwkv_w0, rwkv_w2, rwkv_a0, rwkv_a2, rwkv_g2, rwkv_k_k, rwkv_k_a, rwkv_r_k, rwkv_ln_w, rwkv_ln_b, attn_q_norm, attn_k_norm, w_proj_rwkv, w_proj_attn, w_out, ffn2_norm, ffn2_w_in, ffn2_w_out):
    batch, seq_len, d = x.shape
    params = (ffn1_norm, ffn1_w_in, ffn1_w_out, mix_norm, w_in, b_gate, rwkv_mu, rwkv_w0, rwkv_w2,
              rwkv_a0, rwkv_a2, rwkv_g2, rwkv_k_k, rwkv_k_a, rwkv_r_k, rwkv_ln_w, rwkv_ln_b,
              attn_q_norm, attn_k_norm, w_proj_rwkv, w_proj_attn, w_out, ffn2_norm, ffn2_w_in,
              ffn2_w_out)
    h = x.reshape(batch * seq_len, d)
    for layer in range(ffn1_norm.shape[0]):
        h = _layer(h, seq_len, *(p[layer] for p in params))
    return h.reshape(batch, seq_len, d)
```

```python
import functools

import jax
import jax.numpy as jnp
from jax import lax
from jax.experimental import pallas as pl
from jax.experimental.pallas import tpu as pltpu

F32 = jnp.float32
BF16 = jnp.bfloat16

D_MODEL = 1024
HEAD_DIM = 64
RWKV_WIDTH = 1024
DECAY_LORA = 64
AAA_LORA = 64
GATE_LORA = 160
RWKV_COLS = 3 * RWKV_WIDTH + DECAY_LORA + AAA_LORA + GATE_LORA
GN_EPS = 64e-5
ATTN_PAIRS = ((128, 1), (512, 4), (2048, 16))
ATTN_GROUPS = 3
ATTN_GROUP_WIDTH = 256
ATTN_WIDTH = ATTN_GROUPS * ATTN_GROUP_WIDTH
ATTN_BLOCK = 128
D_FF = 2816
RMS_EPS = 1e-6
NEG_INF = -1e30

LANES = 128
SUBLANES = 8
MXU_DIM = 256
HEADS_PER_TILE = MXU_DIM // HEAD_DIM
VMEM_LIMIT_BYTES = 56 * 1024 * 1024

LORA_PAD = 128
GATE_LORA_PAD = 256
RW_PAD = 3 * RWKV_WIDTH + LORA_PAD + GATE_LORA_PAD
ATT0 = RW_PAD
GATE0 = ATT0 + 3 * ATTN_WIDTH

FF_CHUNK = MXU_DIM
FFN_TILE = 512
PROJ_TILE = 256
CAST_ROWS = 128
MERGE_TILE = 512
WKV_CHUNK = 64
WKV_LOG2_CHUNK = 6
ATTN_UNROLL = 4
WKV_SEQS = 8
WKV_GROUPS = 8
WKV_STAGGER = 1


def _dot(a, b):
    return jnp.dot(a, b, preferred_element_type=F32)


def _dot_nt(a, b):
    return lax.dot_general(a, b, (((1,), (1,)), ((), ())), preferred_element_type=F32)


def _dot_tn(a, b):
    return lax.dot_general(a, b, (((0,), (0,)), ((), ())), preferred_element_type=F32)


def _rms_norm_bf16(x, gain):
    ms = jnp.mean(x * x, axis=-1, keepdims=True)
    return (x * lax.rsqrt(ms + RMS_EPS) * gain).astype(BF16)


def _head_sums(t, ones_bd):
    parts = []
    for q in range(t.shape[-1] // MXU_DIM):
        parts.append(_dot(t[:, q * MXU_DIM:(q + 1) * MXU_DIM].astype(BF16), ones_bd))
    return jnp.concatenate(parts, axis=-1)


def _head_sums_xlane(t):
    low_half = lax.broadcasted_iota(jnp.int32, (1, LANES), 1) < HEAD_DIM
    parts = []
    for j in range(t.shape[-1] // LANES):
        ts = t[:, j * LANES:(j + 1) * LANES]
        s0 = jnp.sum(jnp.where(low_half, ts, 0.0), axis=-1, keepdims=True)
        s1 = jnp.sum(jnp.where(low_half, 0.0, ts), axis=-1, keepdims=True)
        parts.append(jnp.where(low_half, s0, s1))
    return jnp.concatenate(parts, axis=-1)


def _resident(shape):
    return pl.BlockSpec(shape, lambda *_: (0,) * len(shape), pipeline_mode=pl.Buffered(1))


def _ffn_body(x_ref, gain_ref, w_in_ref, w_out_ref, o_ref):
    x = x_ref[...]
    h = _rms_norm_bf16(x, gain_ref[...])
    acc = None
    for c in range(D_FF // FF_CHUNK):
        lo = c * FF_CHUNK
        gate = _dot(h, w_in_ref[:, lo:lo + FF_CHUNK])
        up = _dot(h, w_in_ref[:, D_FF + lo:D_FF + lo + FF_CHUNK])
        act = (gate * jax.nn.sigmoid(gate) * up).astype(BF16)
        part = _dot(act, w_out_ref[lo:lo + FF_CHUNK, :])
        acc = part if acc is None else acc + part
    o_ref[...] = x + 0.5 * acc


def _ffn(x, gain, w_in, w_out):
    t = x.shape[0]
    tm = FFN_TILE
    return pl.pallas_call(
        _ffn_body,
        grid=(t // tm,),
        in_specs=[
            pl.BlockSpec((tm, D_MODEL), lambda i: (i, 0)),
            _resident((1, D_MODEL)),
            _resident((D_MODEL, 2 * D_FF)),
            _resident((D_FF, D_MODEL)),
        ],
        out_specs=pl.BlockSpec((tm, D_MODEL), lambda i: (i, 0)),
        out_shape=jax.ShapeDtypeStruct((t, D_MODEL), F32),
        compiler_params=pltpu.CompilerParams(
            dimension_semantics=("parallel",), vmem_limit_bytes=VMEM_LIMIT_BYTES),
        name="ffn",
    )(x, gain, w_in, w_out)


def _softplus(z):
    return jnp.maximum(z, 0.0) + jnp.log(1.0 + jnp.exp(-jnp.abs(z)))


def _proj_body(tiles_per_seq,
               x_ref, gain_ref, w_ref, wag_ref, bg_ref, mu_ref, w0_ref, w2_ref, a0_ref, a2_ref, g2_ref,
               kk_ref, ka_ref, rk_ref, qg_ref, kg_ref,
               r_o, lw_o, k_o, v_o, kkn_o, b_o, bonus_o, g_o, aq_o, ak_o, av_o, gate_o,
               carry_ref):
    i = pl.program_id(0)
    tm = x_ref.shape[0]
    w = RWKV_WIDTH
    sw = MXU_DIM
    h = _rms_norm_bf16(x_ref[...], gain_ref[...])
    row8 = lax.broadcasted_iota(jnp.int32, (SUBLANES, 1), 0)
    inv_n = 1.0 / HEAD_DIM

    @pl.when((i % tiles_per_seq) == 0)
    def _():
        carry_ref[...] = jnp.zeros_like(carry_ref)

    def mm(lo, hi):
        if hi <= ATT0:
            return _dot(h, w_ref[:, lo:hi])
        return _dot(h, wag_ref[:, lo - ATT0:hi - ATT0])

    def shift(p, lo, hi):
        rolled = pltpu.roll(p, 1, 0)
        head = jnp.where(row8 == 0, carry_ref[:, lo:hi], rolled[:SUBLANES])
        prev = jnp.concatenate([head, rolled[SUBLANES:]], axis=0)
        carry_ref[:, lo:hi] = p[tm - 1:tm, :]
        return p + (prev - p) * mu_ref[:, lo:hi]

    head_sum = _head_sums_xlane

    def rwkv_mm(q):
        c = q * sw
        return [mm(j * w + c, j * w + c + sw) for j in range(3)]

    def rwkv_strip(q, prk, lora16, tanh16, sg16):
        c = q * sw
        cols = slice(c, c + sw)
        r = shift(prk[0], c, c + sw)
        k0 = shift(prk[1], w + c, w + c + sw)
        v = shift(prk[2], 2 * w + c, 2 * w + c + sw)
        wraw = w0_ref[:, cols] + _dot(tanh16, w2_ref[:, cols])
        lw_o[:, cols] = -jnp.exp(-_softplus(-wraw) - 0.5)
        a = jax.nn.sigmoid(a0_ref[:, cols] + _dot(lora16, a2_ref[:, cols]))
        g_o[:, cols] = _dot(sg16, g2_ref[:, cols]).astype(BF16)
        kk = k0 * kk_ref[:, cols]
        kkn = kk * lax.rsqrt(jnp.maximum(head_sum(kk * kk), 1e-24))
        k = k0 * (1.0 + (a - 1.0) * ka_ref[:, cols])
        r_o[:, cols] = r.astype(BF16)
        k_o[:, cols] = k.astype(BF16)
        v_o[:, cols] = v.astype(BF16)
        kkn_o[:, cols] = kkn.astype(BF16)
        b_o[:, cols] = (kkn * a).astype(BF16)
        bonus_o[:, cols] = (head_sum(r * k * rk_ref[:, cols]) * v).astype(BF16)

    def attn_mm(g):
        c = ATT0 + g * sw
        return [mm(c + j * ATTN_WIDTH, c + j * ATTN_WIDTH + sw) for j in range(3)]

    def attn_strip(g, pqkv):
        cols = slice(g * sw, (g + 1) * sw)
        pq, pk, pv = pqkv
        qn = pq * lax.rsqrt(head_sum(pq * pq) * inv_n + RMS_EPS) * qg_ref[:, cols]
        kn = pk * lax.rsqrt(head_sum(pk * pk) * inv_n + RMS_EPS) * kg_ref[:, cols]
        aq_o[:, cols] = qn.astype(BF16)
        ak_o[:, cols] = kn.astype(BF16)
        av_o[:, cols] = pv.astype(BF16)

    def gate_mm(q):
        return mm(GATE0 + q * 2 * sw, GATE0 + (q + 1) * 2 * sw)

    def gate_strip(q, pg):
        cols = slice(q * 2 * sw, (q + 1) * 2 * sw)
        gate_o[:, cols] = jax.nn.sigmoid(pg + bg_ref[:, cols]).astype(BF16)

    p_l = mm(3 * w, RW_PAD)
    nxt = rwkv_mm(0)
    ps_l = shift(p_l, 3 * w, RW_PAD)
    lora = ps_l[:, :LORA_PAD]
    lora16 = lora.astype(BF16)
    tanh16 = jnp.tanh(lora).astype(BF16)
    sg16 = jax.nn.sigmoid(ps_l[:, LORA_PAD:]).astype(BF16)

    n_rw = w // sw
    n_gate = 2 * D_MODEL // (2 * sw)
    for q in range(n_rw):
        cur = nxt
        nxt = rwkv_mm(q + 1) if q + 1 < n_rw else attn_mm(0)
        rwkv_strip(q, cur, lora16, tanh16, sg16)
    for g in range(ATTN_GROUPS):
        cur = nxt
        nxt = attn_mm(g + 1) if g + 1 < ATTN_GROUPS else gate_mm(0)
        attn_strip(g, cur)
    for q in range(n_gate):
        cur = nxt
        nxt = gate_mm(q + 1) if q + 1 < n_gate else None
        gate_strip(q, cur)


def _proj(x, seq_len, gain, w, wag, bg, mu, w0, w2, a0, a2, g2, k_k, k_a, r_k, qg, kg):
    t = x.shape[0]
    tm = PROJ_TILE
    row = lambda n: pl.BlockSpec((tm, n), lambda i: (i, 0))
    wide = jax.ShapeDtypeStruct((t, RWKV_WIDTH), BF16)
    wide32 = jax.ShapeDtypeStruct((t, RWKV_WIDTH), F32)
    att = jax.ShapeDtypeStruct((t, ATTN_WIDTH), BF16)
    return pl.pallas_call(
        functools.partial(_proj_body, seq_len // tm),
        grid=(t // tm,),
        in_specs=[row(D_MODEL)] + [_resident(a.shape) for a in
                                   (gain, w, wag, bg, mu, w0, w2, a0, a2, g2, k_k, k_a, r_k, qg, kg)],
        out_specs=[row(RWKV_WIDTH)] * 8 + [row(ATTN_WIDTH)] * 3 + [row(2 * D_MODEL)],
        out_shape=[wide, wide32] + [wide] * 6 + [att] * 3
                  + [jax.ShapeDtypeStruct((t, 2 * D_MODEL), BF16)],
        scratch_shapes=[pltpu.VMEM((1, RW_PAD), F32)],
        compiler_params=pltpu.CompilerParams(
            dimension_semantics=("arbitrary",), vmem_limit_bytes=VMEM_LIMIT_BYTES),
        name="proj",
    )(x, gain, w, wag, bg, mu, w0, w2, a0, a2, g2, k_k, k_a, r_k, qg, kg)


def _split_cast_body(w_ref, rw_ref, ag_ref):
    aligned = (RWKV_COLS // LANES) * LANES
    rw_ref[:, :aligned] = w_ref[:, :aligned].astype(BF16)
    lane = lax.broadcasted_iota(jnp.int32, (1, RW_PAD - aligned), 1)
    tail = w_ref[:, aligned:RW_PAD]
    rw_ref[:, aligned:] = jnp.where(lane < RWKV_COLS - aligned, tail, 0.0).astype(BF16)
    ag_ref[...] = w_ref[:, RWKV_COLS:].astype(BF16)


def _split_cast(w_in):
    rows, cols = w_in.shape
    tr = CAST_ROWS
    return pl.pallas_call(
        _split_cast_body,
        grid=(rows // tr,),
        in_specs=[pl.BlockSpec((tr, cols), lambda i: (i, 0))],
        out_specs=[pl.BlockSpec((tr, RW_PAD), lambda i: (i, 0)),
                   pl.BlockSpec((tr, cols - RWKV_COLS), lambda i: (i, 0))],
        out_shape=[jax.ShapeDtypeStruct((rows, RW_PAD), BF16),
                   jax.ShapeDtypeStruct((rows, cols - RWKV_COLS), BF16)],
        compiler_params=pltpu.CompilerParams(
            dimension_semantics=("parallel",), vmem_limit_bytes=VMEM_LIMIT_BYTES),
        name="split_cast",
    )(w_in)


def _wkv_stages(seqs, r_ref, lw_ref, k_ref, v_ref, kkn_ref, b_ref, y_ref, z_ref):
    c_len = WKV_CHUNK
    n_tiles = RWKV_WIDTH // MXU_DIM

    trow = lax.broadcasted_iota(jnp.int32, (c_len, 1), 0)
    lane = lax.broadcasted_iota(jnp.int32, (c_len, MXU_DIM), 1)
    rowc = lax.broadcasted_iota(jnp.int32, (c_len, MXU_DIM), 0)
    head_mask = [(lane // HEAD_DIM) == h for h in range(HEADS_PER_TILE)]
    blk_mask = [(lane // c_len) == h for h in range(HEADS_PER_TILE)]
    col_in_blk = lane % c_len
    strict = col_in_blk < rowc
    incl = col_in_blk <= rowc
    eye = jnp.where(col_in_blk == rowc, 1.0, 0.0)
    br = lax.broadcasted_iota(jnp.int32, (MXU_DIM, MXU_DIM), 0) // HEAD_DIM
    bc = lax.broadcasted_iota(jnp.int32, (MXU_DIM, MXU_DIM), 1) // HEAD_DIM
    bd_mask = br == bc

    def stack(x, masks):
        return jnp.concatenate([jnp.where(m, x, 0.0) for m in masks], axis=0).astype(BF16)

    def rows(*xs):
        return jnp.concatenate(xs, axis=0)

    units = []
    for s in seqs:
        lw = lw_ref[s]
        cum = lw
        for sh in (1, 2, 4, 8, 16, 32):
            cum = cum + jnp.where(trow >= sh, pltpu.roll(cum, sh, 0), 0.0)
        ref = cum[c_len // 2 - 1:c_len // 2, :]
        e_fwd = jnp.exp(cum - ref)
        e_prev = jnp.exp(cum - lw - ref)
        e_bwd = jnp.exp(ref - cum)
        e_ref = jnp.exp(ref)
        lam_end = e_fwd[c_len - 1:c_len, :]
        lam_c = jnp.exp(cum[c_len - 1:c_len, :])
        at = -kkn_ref[s].astype(F32) * e_prev
        rt = r_ref[s].astype(F32) * e_fwd
        kt = k_ref[s].astype(F32) * e_bwd
        bt = b_ref[s].astype(F32) * e_bwd
        v = v_ref[s].astype(F32)
        for q in range(n_tiles):
            sl = slice(q * MXU_DIM, (q + 1) * MXU_DIM)
            units.append(dict(
                s=s, sl=sl, zi=s * n_tiles + q, at=at[:, sl], rt=rt[:, sl], kt=kt[:, sl],
                bt=bt[:, sl], v=v[:, sl], e_ref=e_ref[:, sl], lam_end=lam_end[:, sl],
                lam_c=lam_c[:, sl]))
        yield

    for u in units:
        ar = rows(u["at"], u["rt"]).astype(BF16)
        u["sb"] = _dot_nt(ar, stack(u["bt"], head_mask))
        u["sk"] = _dot_nt(ar, stack(u["kt"], head_mask))
    yield
    for u in units:
        u["a_ab"] = jnp.where(strict, u["sb"][:c_len], 0.0)
        u["a_rb"] = jnp.where(incl, u["sb"][c_len:], 0.0).astype(BF16)
        a_ak = jnp.where(strict, u["sk"][:c_len], 0.0)
        a_rk = jnp.where(incl, u["sk"][c_len:], 0.0)
        u["xv"] = _dot(rows(a_ak, a_rk).astype(BF16), stack(u["v"], head_mask))
        u["pw"] = _dot(u["a_ab"].astype(BF16), stack(u["a_ab"], blk_mask))
        u["t_inv"] = eye + u["a_ab"]
    yield
    for lev in range(1, WKV_LOG2_CHUNK):
        for u in units:
            rhs = stack(u["pw"], blk_mask)
            if lev < WKV_LOG2_CHUNK - 1:
                prod = _dot(rows(u["pw"], u["t_inv"]).astype(BF16), rhs)
                u["pw"] = prod[:c_len]
                u["t_inv"] = u["t_inv"] + prod[c_len:]
            else:
                u["t_inv"] = u["t_inv"] + _dot(u["t_inv"].astype(BF16), rhs)
        yield
    for u in units:
        z16 = z_ref[u["zi"]].astype(BF16)
        ra = rows(u["rt"] * u["e_ref"], u["at"] * u["e_ref"]).astype(BF16)
        zs = _dot_nt(ra, z16)
        u["ys"] = zs[:c_len]
        u["rhs_u"] = zs[c_len:] + u["xv"][:c_len]
    yield
    for u in units:
        u["uu"] = _dot(u["t_inv"].astype(BF16), stack(u["rhs_u"], head_mask))
    yield
    for u in units:
        y = u["ys"] + _dot(u["a_rb"], stack(u["uu"], head_mask)) + u["xv"][c_len:]
        y_ref[u["s"], :, u["sl"]] = y.astype(BF16)
        bh = (u["bt"] * u["lam_end"]).astype(BF16)
        kh = (u["kt"] * u["lam_end"]).astype(BF16)
        g = _dot_tn(rows(u["uu"], u["v"]).astype(BF16), rows(bh, kh))
        z_ref[u["zi"]] = z_ref[u["zi"]] * u["lam_c"] + jnp.where(bd_mask, g, 0.0)


def _wkv_body(*refs):
    z_ref = refs[-1]
    n_seq = refs[0].shape[0]

    @pl.when(pl.program_id(1) == 0)
    def _():
        z_ref[...] = jnp.zeros_like(z_ref)

    n_groups = min(WKV_GROUPS, n_seq)
    per = n_seq // n_groups
    gens = [_wkv_stages(range(g * per, (g + 1) * per), *refs) for g in range(n_groups)]
    live = [True] * n_groups
    tick = 0
    while any(live):
        for g, gen in enumerate(gens):
            if live[g] and tick >= g * WKV_STAGGER:
                live[g] = next(gen, "done") != "done"
        tick += 1


def _wkv(batch, seq_len, r, lw, k, v, kkn, b):
    n_chunks = seq_len // WKV_CHUNK
    n_seq = WKV_SEQS if batch % WKV_SEQS == 0 else 1
    view = lambda a: a.reshape(batch, seq_len, RWKV_WIDTH)
    blk = pl.BlockSpec((n_seq, WKV_CHUNK, RWKV_WIDTH), lambda bi, ci: (bi, ci, 0))
    y = pl.pallas_call(
        _wkv_body,
        grid=(batch // n_seq, n_chunks),
        in_specs=[blk] * 6,
        out_specs=blk,
        out_shape=jax.ShapeDtypeStruct((batch, seq_len, RWKV_WIDTH), BF16),
        scratch_shapes=[pltpu.VMEM((n_seq * RWKV_WIDTH // MXU_DIM, MXU_DIM, MXU_DIM), F32)],
        compiler_params=pltpu.CompilerParams(
            dimension_semantics=("arbitrary", "arbitrary"), vmem_limit_bytes=VMEM_LIMIT_BYTES),
        name="wkv",
    )(*(view(a) for a in (r, lw, k, v, kkn, b)))
    return y.reshape(batch * seq_len, RWKV_WIDTH)


def _phase_perm(dil):
    n = MXU_DIM // dil
    out = lax.broadcasted_iota(jnp.int32, (MXU_DIM, MXU_DIM), 0)
    src = lax.broadcasted_iota(jnp.int32, (MXU_DIM, MXU_DIM), 1)
    return jnp.where(src == (out % n) * dil + out // n, 1.0, 0.0).astype(BF16)


def _attn_regroup(seq_len, dil, q_ref, k_ref, v_ref, qsub, ksub, vsub):
    blk = ATTN_BLOCK
    sub = seq_len // dil
    n = MXU_DIM // dil
    ksub[0:blk, :] = jnp.zeros((blk, ATTN_GROUP_WIDTH), BF16)
    vsub[0:blk, :] = jnp.zeros((blk, ATTN_GROUP_WIDTH), BF16)
    perm = None if dil == 1 else _phase_perm(dil)
    for src, dst, off in ((q_ref, qsub, 0), (k_ref, ksub, blk), (v_ref, vsub, blk)):
        if dil == 1:
            dst[off:off + seq_len, :] = src[...]
            continue
        for t in range(seq_len // MXU_DIM):
            y = _dot(perm, src[t * MXU_DIM:(t + 1) * MXU_DIM, :]).astype(BF16)
            for c in range(dil):
                dst[off + c * sub + t * n:off + c * sub + (t + 1) * n, :] = y[c * n:(c + 1) * n]


def _attn_scatter(seq_len, dil, osub, lsub, o_ref, l_ref):
    sub = seq_len // dil
    n = MXU_DIM // dil
    if dil == 1:
        o_ref[...] = osub[...]
        l_ref[...] = lsub[...]
        return
    perm = _phase_perm(dil)
    for t in range(seq_len // MXU_DIM):
        rows = lambda ref: jnp.concatenate(
            [ref[c * sub + t * n:c * sub + (t + 1) * n, :] for c in range(dil)], axis=0)
        out = slice(t * MXU_DIM, (t + 1) * MXU_DIM)
        o_ref[out, :] = _dot_tn(perm, rows(osub)).astype(BF16)
        l = rows(lsub)
        hi = l.astype(BF16)
        rem = l - hi.astype(F32)
        mid = rem.astype(BF16)
        lo = (rem - mid.astype(F32)).astype(BF16)
        l_ref[out, :] = _dot_tn(perm, hi) + _dot_tn(perm, mid) + _dot_tn(perm, lo)


def _attn_body(seq_len, q_ref, k_ref, v_ref, o_ref, l_ref, qsub, ksub, vsub, osub, lsub):
    g = pl.program_id(1)
    blk = ATTN_BLOCK
    n_blocks = seq_len // blk

    for gi, (_, dil) in enumerate(ATTN_PAIRS):
        pl.when(g == gi)(functools.partial(
            _attn_regroup, seq_len, dil, q_ref, k_ref, v_ref, qsub, ksub, vsub))

    blocks_per_sub = jnp.int32(n_blocks)
    for gi, (_, dil) in enumerate(ATTN_PAIRS):
        blocks_per_sub = jnp.where(g == gi, n_blocks // dil, blocks_per_sub)

    qi = lax.broadcasted_iota(jnp.int32, (blk, 2 * blk), 0)
    kj = lax.broadcasted_iota(jnp.int32, (blk, 2 * blk), 1)
    dist = qi + blk - kj
    bias_any = jnp.where((dist >= 0) & (dist <= blk), 0.0, NEG_INF)
    bias_first = jnp.where(kj >= blk, bias_any, NEG_INF)
    lane = lax.broadcasted_iota(jnp.int32, (blk, ATTN_GROUP_WIDTH), 1)
    head_mask = [(lane // HEAD_DIM) == h for h in range(HEADS_PER_TILE)]

    def blocks(it, carry):
        ids = [it * ATTN_UNROLL + j for j in range(ATTN_UNROLL)]
        r0s = [pl.multiple_of(i * blk, blk) for i in ids]
        ss, ps, dens, ms, os_ = [], [], [], [], []
        for i, r0 in zip(ids, r0s):
            q = qsub[pl.ds(r0, blk), :]
            qs = jnp.concatenate([jnp.where(m, q, jnp.zeros_like(q)) for m in head_mask], axis=0)
            bias = jnp.where((i % blocks_per_sub) > 0, bias_any, bias_first)
            s_ = _dot_nt(qs, ksub[pl.ds(r0, 2 * blk), :])
            ss.append((s_.reshape(HEADS_PER_TILE, blk, 2 * blk) + bias[None]).reshape(s_.shape))
        for s_ in ss:
            ms.append(jnp.max(s_, axis=-1, keepdims=True))
        for s_, m in zip(ss, ms):
            ps.append(jnp.exp(s_ - m))
        for p in ps:
            dens.append(jnp.sum(p, axis=-1, keepdims=True))
        for p, r0 in zip(ps, r0s):
            os_.append(_dot(p.astype(BF16), vsub[pl.ds(r0, 2 * blk), :]))
        for o_all, den, m, r0 in zip(os_, dens, ms, r0s):
            o_all = o_all * (1.0 / den)
            lse = m + jnp.log(den)
            o = jnp.zeros((blk, ATTN_GROUP_WIDTH), F32)
            l = jnp.zeros((blk, ATTN_GROUP_WIDTH), F32)
            for h in range(HEADS_PER_TILE):
                rows = slice(h * blk, (h + 1) * blk)
                o = jnp.where(head_mask[h], o_all[rows], o)
                l = jnp.where(head_mask[h], lse[rows], l)
            osub[pl.ds(r0, blk), :] = o.astype(BF16)
            lsub[pl.ds(r0, blk), :] = l
        return carry

    lax.fori_loop(0, n_blocks // ATTN_UNROLL, blocks, 0)

    for gi, (_, dil) in enumerate(ATTN_PAIRS):
        pl.when(g == gi)(functools.partial(_attn_scatter, seq_len, dil, osub, lsub, o_ref, l_ref))


def _attn(batch, seq_len, aq, ak, av):
    view = lambda a: a.reshape(batch, seq_len, ATTN_WIDTH)
    spec = pl.BlockSpec((None, seq_len, ATTN_GROUP_WIDTH), lambda bi, g: (bi, 0, g))
    sub_rows = seq_len + ATTN_BLOCK
    o, l = pl.pallas_call(
        functools.partial(_attn_body, seq_len),
        grid=(batch, ATTN_GROUPS),
        in_specs=[spec] * 3,
        out_specs=[spec] * 2,
        out_shape=[jax.ShapeDtypeStruct((batch, seq_len, ATTN_WIDTH), BF16),
                   jax.ShapeDtypeStruct((batch, seq_len, ATTN_WIDTH), F32)],
        scratch_shapes=[
            pltpu.VMEM((seq_len, ATTN_GROUP_WIDTH), BF16),
            pltpu.VMEM((sub_rows, ATTN_GROUP_WIDTH), BF16),
            pltpu.VMEM((sub_rows, ATTN_GROUP_WIDTH), BF16),
            pltpu.VMEM((seq_len, ATTN_GROUP_WIDTH), BF16),
            pltpu.VMEM((seq_len, ATTN_GROUP_WIDTH), F32),
        ],
        compiler_params=pltpu.CompilerParams(
            dimension_semantics=("parallel", "arbitrary"), vmem_limit_bytes=VMEM_LIMIT_BYTES),
        name="attn",
    )(view(aq), view(ak), view(av))
    return o.reshape(batch * seq_len, ATTN_WIDTH), l.reshape(batch * seq_len, ATTN_WIDTH)


def _merge_body(x_ref, wkv_ref, bonus_ref, g_ref, gate_ref, o_ref, l_ref,
                lnw_ref, lnb_ref, ones_ref, wpr_ref, wpa_ref, wo_ref, out_ref):
    ones_bd = ones_ref[...]
    inv_n = 1.0 / HEAD_DIM
    wkv = wkv_ref[...].astype(F32)
    dev = wkv - _head_sums(wkv, ones_bd) * inv_n
    var = _head_sums(dev * dev, ones_bd) * inv_n
    y = dev * lax.rsqrt(var + GN_EPS) * lnw_ref[...] + lnb_ref[...]
    y_a = ((y + bonus_ref[...].astype(F32)) * g_ref[...].astype(F32)).astype(BF16)
    pa = _dot(y_a, wpr_ref[...])
    gw = ATTN_GROUP_WIDTH
    l0, l1, l2 = l_ref[:, 0:gw], l_ref[:, gw:2 * gw], l_ref[:, 2 * gw:3 * gw]
    mx = jnp.maximum(jnp.maximum(l0, l1), l2)
    e0, e1, e2 = jnp.exp(l0 - mx), jnp.exp(l1 - mx), jnp.exp(l2 - mx)
    inv = 1.0 / (e0 + e1 + e2)
    pb = None
    for gi, e in enumerate((e0, e1, e2)):
        yb = (o_ref[:, gi * gw:(gi + 1) * gw].astype(F32) * (e * inv)).astype(BF16)
        part = _dot(yb, wpa_ref[gi * gw:(gi + 1) * gw, :])
        pb = part if pb is None else pb + part
    gates = gate_ref[...].astype(F32)
    merged = gates[:, :D_MODEL] * pa + gates[:, D_MODEL:] * pb
    out_ref[...] = x_ref[...] + _dot(merged.astype(BF16), wo_ref[...])


def _merge(x, wkv, bonus, g, gates, attn_o, attn_l, ln_w, ln_b, ones_bd, wpr, wpa, wo):
    t = x.shape[0]
    tm = MERGE_TILE
    row = lambda n: pl.BlockSpec((tm, n), lambda i: (i, 0))
    small = (ln_w, ln_b, ones_bd, wpr, wpa, wo)
    return pl.pallas_call(
        _merge_body,
        grid=(t // tm,),
        in_specs=[row(D_MODEL)] * 4 + [row(2 * D_MODEL)] + [row(ATTN_WIDTH)] * 2
                 + [_resident(a.shape) for a in small],
        out_specs=row(D_MODEL),
        out_shape=jax.ShapeDtypeStruct((t, D_MODEL), F32),
        compiler_params=pltpu.CompilerParams(
            dimension_semantics=("parallel",), vmem_limit_bytes=VMEM_LIMIT_BYTES),
        name="merge",
    )(x, wkv, bonus, g, gates, attn_o, attn_l, *small)


def _pad_rows(a, n):
    return jnp.pad(a, ((0, n - a.shape[0]), (0, 0)))


def _layer(x, seq_len, ffn1_norm, ffn1_w_in, ffn1_w_out, mix_norm, w_in, b_gate, rwkv_mu,
           rwkv_w0, rwkv_w2, rwkv_a0, rwkv_a2, rwkv_g2, rwkv_k_k, rwkv_k_a, rwkv_r_k,
           rwkv_ln_w, rwkv_ln_b, attn_q_norm, attn_k_norm, w_proj_rwkv, w_proj_attn,
           w_out, ffn2_norm, ffn2_w_in, ffn2_w_out):
    batch = x.shape[0] // seq_len
    row = lambda a: a.reshape(1, -1)

    w_rw, w_ag = _split_cast(w_in)
    mu_pad = jnp.pad(rwkv_mu, (0, RW_PAD - RWKV_COLS)).reshape(1, RW_PAD)
    w2_pad = _pad_rows(rwkv_w2, LORA_PAD).astype(BF16)
    a2_pad = jnp.pad(rwkv_a2, ((DECAY_LORA, 0), (0, 0))).astype(BF16)
    g2_pad = _pad_rows(rwkv_g2, GATE_LORA_PAD).astype(BF16)
    n_heads = ATTN_WIDTH // HEAD_DIM
    q_gain = row(jnp.tile(attn_q_norm, n_heads) * HEAD_DIM ** -0.5)
    k_gain = row(jnp.tile(attn_k_norm, n_heads))
    idx = jnp.arange(MXU_DIM) // HEAD_DIM
    ones_bd = (idx[:, None] == idx[None, :]).astype(BF16)

    x = _ffn(x, row(ffn1_norm), ffn1_w_in.astype(BF16), ffn1_w_out.astype(BF16))
    (r, lw, k, v, kkn, b, bonus, g, aq, ak, av, gates) = _proj(
        x, seq_len, row(mix_norm), w_rw, w_ag, row(b_gate), mu_pad, row(rwkv_w0), w2_pad,
        row(rwkv_a0), a2_pad, g2_pad, row(rwkv_k_k), row(rwkv_k_a), row(rwkv_r_k),
        q_gain, k_gain)
    wkv = _wkv(batch, seq_len, r, lw, k, v, kkn, b)
    attn_o, attn_l = _attn(batch, seq_len, aq, ak, av)
    x = _merge(x, wkv, bonus, g, gates, attn_o, attn_l, row(rwkv_ln_w), row(rwkv_ln_b), ones_bd,
               w_proj_rwkv.astype(BF16), w_proj_attn.astype(BF16), w_out.astype(BF16))
    return _ffn(x, row(ffn2_norm), ffn2_w_in.astype(BF16), ffn2_w_out.astype(BF16))


def kernel(x, ffn1_norm, ffn1_w_in, ffn1_w_out, mix_norm, w_in, b_gate, rwkv_mu, rwkv_w0, rwkv_w2, rwkv_a0, rwkv_a2, rwkv_g2, rwkv_k_k, rwkv_k_a, rwkv_r_k, rwkv_ln_w, rwkv_ln_b, attn_q_norm, attn_k_norm, w_proj_rwkv, w_proj_attn, w_out, ffn2_norm, ffn2_w_in, ffn2_w_out):
    batch, seq_len, d = x.shape
    params = (ffn1_norm, ffn1_w_in, ffn1_w_out, mix_norm, w_in, b_gate, rwkv_mu, rwkv_w0, rwkv_w2,
              rwkv_a0, rwkv_a2, rwkv_g2, rwkv_k_k, rwkv_k_a, rwkv_r_k, rwkv_ln_w, rwkv_ln_b,
              attn_q_norm, attn_k_norm, w_proj_rwkv, w_proj_attn, w_out, ffn2_norm, ffn2_w_in,
              ffn2_w_out)
    h = x.reshape(batch * seq_len, d)
    for layer in range(ffn1_norm.shape[0]):
        h = _layer(h, seq_len, *(p[layer] for p in params))
    return h.reshape(batch, seq_len, d)
```
